```python
import jax, jax.numpy as jnp
from jax import lax
import numpy as np

D_MODEL = 4096
BATCH = 4
SEQ = 4096
DEPTH = 1

GRID_W = 64
CTX_LEN = 256
LRU_WIDTH = D_MODEL
LRU_HEADS = 16
LRU_BLOCK = LRU_WIDTH // LRU_HEADS
LRU_CONV = 4
LRU_C = 8.0
CONF_WIDTH = D_MODEL
CONF_K = 31
D_FF = 4 * D_MODEL
N_BRANCH = 2
EPS = 1e-6
PROJ_COLS = 2 * LRU_WIDTH + 2 * CONF_WIDTH + N_BRANCH * D_MODEL

kernel_name = 'hybrid_rglru_conformer_dit_block'


def rmsnorm(x, g):
    xf = x.astype(jnp.float32)
    y = xf * lax.rsqrt(jnp.mean(xf * xf, axis=-1, keepdims=True) + EPS)
    return (y * g.astype(jnp.float32)).astype(x.dtype)


def layernorm(x, g, b):
    xf = x.astype(jnp.float32)
    mu = jnp.mean(xf, axis=-1, keepdims=True)
    xc = xf - mu
    y = xc * lax.rsqrt(jnp.mean(xc * xc, axis=-1, keepdims=True) + EPS)
    return (y * g.astype(jnp.float32) + b.astype(jnp.float32)).astype(x.dtype)


def modulate(h, shift, scale):
    return h * (1 + scale) + shift


def depthwise_conv1d(x, w, b, pad):
    y = lax.conv_general_dilated(x, w[:, None, :], window_strides=(1,), padding=[pad],
                                 dimension_numbers=('NWC', 'WIO', 'NWC'),
                                 feature_group_count=x.shape[-1])
    return y + b


def rglru_coeffs(y, w_a, b_a, w_i, b_i, lam):
    bsz, t, r = y.shape
    yh = y.reshape(bsz, t, LRU_HEADS, LRU_BLOCK)
    gate_r = jax.nn.sigmoid(jnp.einsum('bthi,hij->bthj', yh, w_a).reshape(bsz, t, r).astype(jnp.float32) + b_a)
    gate_i = jax.nn.sigmoid(jnp.einsum('bthi,hij->bthj', yh, w_i).reshape(bsz, t, r).astype(jnp.float32) + b_i)
    log_a = -LRU_C * gate_r * jax.nn.softplus(-lam.astype(jnp.float32))
    a = jnp.exp(log_a)
    mult = jnp.sqrt(-jnp.expm1(2.0 * log_a))
    return a, mult * gate_i * y.astype(jnp.float32)


def _combine(e1, e2):
    a1, b1 = e1
    a2, b2 = e2
    return a1 * a2, a2 * b1 + b2


def rglru_direction(xl, xc, conv_w, conv_b, w_a, b_a, w_i, b_i, lam):
    pad = (LRU_CONV - 1, 0)
    a_c, b_c = rglru_coeffs(depthwise_conv1d(xc, conv_w, conv_b, pad), w_a, b_a, w_i, b_i, lam)
    _, h_c = lax.associative_scan(_combine, (a_c, b_c), axis=1)
    a_l, b_l = rglru_coeffs(depthwise_conv1d(xl, conv_w, conv_b, pad), w_a, b_a, w_i, b_i, lam)
    a_cum, h_l0 = lax.associative_scan(_combine, (a_l, b_l), axis=1)
    h_l = a_cum * h_c[:, -1:, :] + h_l0
    return h_l, h_c


def conformer_conv(glu_in, dw_w, dw_b, ln_g, ln_b, w_out, grid):
    v = glu_in[..., :CONF_WIDTH] * jax.nn.sigmoid(glu_in[..., CONF_WIDTH:])
    pad = ((CONF_K - 1) // 2, (CONF_K - 1) // 2)
    if grid:
        bsz, t, ch = v.shape
        rows = t // GRID_W
        half = ch // 2
        vh = v[..., :half].reshape(bsz * rows, GRID_W, half)
        yh = depthwise_conv1d(vh, dw_w[:, :half], dw_b[:half], pad).reshape(bsz, t, half)
        vv = v[..., half:].reshape(bsz, rows, GRID_W, half).transpose(0, 2, 1, 3).reshape(bsz * GRID_W, rows, half)
        yv = depthwise_conv1d(vv, dw_w[:, half:], dw_b[half:], pad)
        yv = yv.reshape(bsz, GRID_W, rows, half).transpose(0, 2, 1, 3).reshape(bsz, t, half)
        v = jnp.concatenate([yh, yv], axis=-1)
    else:
        v = depthwise_conv1d(v, dw_w, dw_b, pad)
    return jax.nn.silu(layernorm(v, ln_g, ln_b)) @ w_out


def gated_merge(lru_out, conf_out, gate_logits, w_o):
    g = jax.nn.sigmoid(gate_logits)
    g_lru, g_conf = jnp.split(g, N_BRANCH, axis=-1)
    return (g_lru * lru_out + g_conf * conf_out) @ w_o


def sq_relu_mlp(u, w1, w2):
    return jnp.square(jax.nn.relu(u @ w1)) @ w2


def hybrid_layer(x, ctx, c_silu, ctx_silu, w_ada, b_ada, norm1_g, w_in,
                 lru_conv_w, lru_conv_b, lru_w_a, lru_b_a, lru_w_i, lru_b_i, lru_lam,
                 w_lru_out, conf_dw_w, conf_dw_b, conf_ln_g, conf_ln_b, w_conf_out,
                 w_o, norm2_g, w_ff1, w_ff2, update_ctx):
    splits = [LRU_WIDTH, 2 * LRU_WIDTH, 2 * LRU_WIDTH + 2 * CONF_WIDTH]
    mod = (c_silu @ w_ada + b_ada)[:, None, :]
    mod_c = ctx_silu @ w_ada + b_ada
    sh1, sc1, g1, sh2, sc2, g2 = jnp.split(mod, 6, axis=-1)
    csh1, csc1, cg1, csh2, csc2, cg2 = jnp.split(mod_c, 6, axis=-1)

    u = modulate(rmsnorm(x, norm1_g), sh1, sc1)
    uc = modulate(rmsnorm(ctx, norm1_g), csh1, csc1)

    gate_br, x_br, glu_in, br_logits = jnp.split(u @ w_in, splits, axis=-1)
    if update_ctx:
        gate_c, xc_br, glu_c, logits_c = jnp.split(uc @ w_in, splits, axis=-1)
    else:
        xc_br = uc @ w_in[:, LRU_WIDTH:2 * LRU_WIDTH]

    h_l_f, h_c_f = rglru_direction(x_br, xc_br, lru_conv_w[0], lru_conv_b[0], lru_w_a[0], lru_b_a[0],
                                   lru_w_i[0], lru_b_i[0], lru_lam[0])
    h_l_b, h_c_b = rglru_direction(jnp.flip(x_br, axis=1), jnp.flip(xc_br, axis=1), lru_conv_w[1], lru_conv_b[1],
                                   lru_w_a[1], lru_b_a[1], lru_w_i[1], lru_b_i[1], lru_lam[1])
    h_l = (h_l_f + jnp.flip(h_l_b, axis=1)).astype(x.dtype)
    lru_out = (h_l * jax.nn.gelu(gate_br)) @ w_lru_out
    conf_out = conformer_conv(glu_in, conf_dw_w, conf_dw_b, conf_ln_g, conf_ln_b, w_conf_out, True)
    x = x + g1 * gated_merge(lru_out, conf_out, br_logits, w_o)
    x = x + g2 * sq_relu_mlp(modulate(rmsnorm(x, norm2_g), sh2, sc2), w_ff1, w_ff2)

    if update_ctx:
        h_c = (h_c_f + jnp.flip(h_c_b, axis=1)).astype(ctx.dtype)
        lru_c = (h_c * jax.nn.gelu(gate_c)) @ w_lru_out
        conf_c = conformer_conv(glu_c, conf_dw_w, conf_dw_b, conf_ln_g, conf_ln_b, w_conf_out, False)
        ctx = ctx + cg1 * gated_merge(lru_c, conf_c, logits_c, w_o)
        ctx = ctx + cg2 * sq_relu_mlp(modulate(rmsnorm(ctx, norm2_g), csh2, csc2), w_ff1, w_ff2)
    return x, ctx


def setup_inputs(seed: int = 0) -> dict:
    key = jax.random.key(seed)
    ks = jax.random.split(key, 26)
    D, R, C = D_MODEL, LRU_WIDTH, CONF_WIDTH
    nrm = jax.random.normal
    a0 = jax.random.uniform(ks[14], (DEPTH, 2, R), minval=0.9, maxval=0.999)
    return {
        'x': nrm(ks[0], (BATCH, SEQ, D), jnp.float32),
        'c': nrm(ks[1], (BATCH, D), jnp.float32),
        'ctx': nrm(ks[2], (BATCH, CTX_LEN, D), jnp.float32),
        'c_ctx': nrm(ks[3], (D,), jnp.float32),
        'w_ada': nrm(ks[4], (DEPTH, D, 6 * D), jnp.float32) * (0.5 * D ** -0.5),
        'b_ada': nrm(ks[5], (DEPTH, 6 * D), jnp.float32) * 0.02,
        'norm1_g': 1.0 + 0.05 * nrm(ks[6], (DEPTH, D), jnp.float32),
        'w_in': nrm(ks[7], (DEPTH, D, PROJ_COLS), jnp.float32) * D ** -0.5,
        'lru_conv_w': nrm(ks[8], (DEPTH, 2, LRU_CONV, R), jnp.float32) * LRU_CONV ** -0.5,
        'lru_conv_b': nrm(ks[9], (DEPTH, 2, R), jnp.float32) * 0.02,
        'lru_w_a': nrm(ks[10], (DEPTH, 2, LRU_HEADS, LRU_BLOCK, LRU_BLOCK), jnp.float32) * LRU_BLOCK ** -0.5,
        'lru_b_a': nrm(ks[11], (DEPTH, 2, R), jnp.float32) * 0.02,
        'lru_w_i': nrm(ks[12], (DEPTH, 2, LRU_HEADS, LRU_BLOCK, LRU_BLOCK), jnp.float32) * LRU_BLOCK ** -0.5,
        'lru_b_i': nrm(ks[13], (DEPTH, 2, R), jnp.float32) * 0.02,
        'lru_lam': jnp.log(a0) - jnp.log1p(-a0),
        'w_lru_out': nrm(ks[15], (DEPTH, R, D), jnp.float32) * R ** -0.5,
        'conf_dw_w': nrm(ks[16], (DEPTH, CONF_K, C), jnp.float32) * CONF_K ** -0.5,
        'conf_dw_b': nrm(ks[17], (DEPTH, C), jnp.float32) * 0.02,
        'conf_ln_g': 1.0 + 0.05 * nrm(ks[18], (DEPTH, C), jnp.float32),
        'conf_ln_b': nrm(ks[19], (DEPTH, C), jnp.float32) * 0.02,
        'w_conf_out': nrm(ks[20], (DEPTH, C, D), jnp.float32) * C ** -0.5,
        'w_o': nrm(ks[21], (DEPTH, D, D), jnp.float32) * D ** -0.5,
        'norm2_g': 1.0 + 0.05 * nrm(ks[22], (DEPTH, D), jnp.float32),
        'w_ff1': nrm(ks[23], (DEPTH, D, D_FF), jnp.float32) * D ** -0.5,
        'w_ff2': nrm(ks[24], (DEPTH, D_FF, D), jnp.float32) * D_FF ** -0.5,
        'final_g': 1.0 + 0.05 * nrm(ks[25], (D,), jnp.float32),
    }


def reference(x, c, ctx, c_ctx, w_ada, b_ada, norm1_g, w_in, lru_conv_w, lru_conv_b,
              lru_w_a, lru_b_a, lru_w_i, lru_b_i, lru_lam, w_lru_out, conf_dw_w, conf_dw_b,
              conf_ln_g, conf_ln_b, w_conf_out, w_o, norm2_g, w_ff1, w_ff2, final_g):
    c_silu = jax.nn.silu(c)
    ctx_silu = jax.nn.silu(c_ctx)
    for layer in range(DEPTH):
        x, ctx = hybrid_layer(x, ctx, c_silu, ctx_silu, w_ada[layer], b_ada[layer], norm1_g[layer], w_in[layer],
                              lru_conv_w[layer], lru_conv_b[layer], lru_w_a[layer], lru_b_a[layer],
                              lru_w_i[layer], lru_b_i[layer], lru_lam[layer], w_lru_out[layer],
                              conf_dw_w[layer], conf_dw_b[layer], conf_ln_g[layer], conf_ln_b[layer],
                              w_conf_out[layer], w_o[layer], norm2_g[layer], w_ff1[layer], w_ff2[layer],
                              layer < DEPTH - 1)
    return rmsnorm(x, final_g)
```

```python
import functools

import jax
import jax.numpy as jnp
from jax import lax
from jax.experimental import pallas as pl
from jax.experimental.pallas import tpu as pltpu

_F32 = jnp.float32
_BF16 = jnp.bfloat16

_GRID_W = 64
_LRU_HEADS = 16
_LRU_CONV = 4
_LRU_C = 8.0
_CONF_K = 31
_CONF_PAD = (_CONF_K - 1) // 2
_EPS = 1e-6

_V7X_SUBLANES = 8
_V7X_SCOPED_VMEM_BYTES = 60000 * 1024


_COMPILER_SCRATCH_BYTES = 2 << 20


def _vmem_limit(nbytes):
    return int(min(nbytes + _COMPILER_SCRATCH_BYTES, _V7X_SCOPED_VMEM_BYTES))


def _nbytes(shape, dtype):
    n = 1
    for s in shape:
        n *= s
    return n * jnp.dtype(dtype).itemsize


def _ada_kernel(cc_ref, w_ref, b_ref, o_ref):
    cc = cc_ref[...]
    s = cc * jax.nn.sigmoid(cc)
    o_ref[...] = jnp.dot(s.astype(_BF16), w_ref[...].astype(_BF16),
                         preferred_element_type=_F32) + b_ref[...]


def _ada(cc, w, b, tn=512):
    m, d = cc.shape
    n = w.shape[1]
    est = 2 * _nbytes((d, tn), _F32) + _nbytes((d, tn), _BF16) + 4 * _nbytes((m, d), _F32) + (4 << 20)
    return pl.pallas_call(
        _ada_kernel,
        grid=(n // tn,),
        in_specs=[pl.BlockSpec((m, d), lambda j: (0, 0)),
                  pl.BlockSpec((d, tn), lambda j: (0, j)),
                  pl.BlockSpec((1, tn), lambda j: (0, j))],
        out_specs=pl.BlockSpec((m, tn), lambda j: (0, j)),
        out_shape=jax.ShapeDtypeStruct((m, n), _F32),
        compiler_params=pltpu.CompilerParams(dimension_semantics=("parallel",),
                                             vmem_limit_bytes=_vmem_limit(est)),
        name="ada_ln",
    )(cc, w, b)


def _norm_mod_kernel(x_ref, g_ref, sc_ref, sh_ref, o_ref):
    x = x_ref[0]
    ms = jnp.mean(x * x, axis=-1, keepdims=True)
    y = x * lax.rsqrt(ms + _EPS) * g_ref[...]
    o_ref[0] = (y * (1.0 + sc_ref[0]) + sh_ref[0]).astype(o_ref.dtype)


def _norm_mod(x, g, sc, sh, tt=512):
    b, t, d = x.shape
    tt = min(tt, t)
    est = 2 * _nbytes((tt, d), _F32) + 2 * _nbytes((tt, d), _BF16) + 2 * _nbytes((tt, d), _F32)
    return pl.pallas_call(
        _norm_mod_kernel,
        grid=(b, t // tt),
        in_specs=[pl.BlockSpec((1, tt, d), lambda i, j: (i, j, 0)),
                  pl.BlockSpec((1, d), lambda i, j: (0, 0)),
                  pl.BlockSpec((1, 1, d), lambda i, j: (i, 0, 0)),
                  pl.BlockSpec((1, 1, d), lambda i, j: (i, 0, 0))],
        out_specs=pl.BlockSpec((1, tt, d), lambda i, j: (i, j, 0)),
        out_shape=jax.ShapeDtypeStruct((b, t, d), _BF16),
        compiler_params=pltpu.CompilerParams(dimension_semantics=("parallel", "parallel"),
                                             vmem_limit_bytes=_vmem_limit(est)),
        name="rmsnorm_modulate",
    )(x, g, sc, sh)


def _final_norm_kernel(x_ref, g_ref, o_ref):
    x = x_ref[...]
    ms = jnp.mean(x * x, axis=-1, keepdims=True)
    o_ref[...] = x * lax.rsqrt(ms + _EPS) * g_ref[...]


def _final_norm(x, g, tt=512):
    m, d = x.shape
    est = 4 * _nbytes((tt, d), _F32) + 2 * _nbytes((tt, d), _F32)
    return pl.pallas_call(
        _final_norm_kernel,
        grid=(m // tt,),
        in_specs=[pl.BlockSpec((tt, d), lambda i: (i, 0)),
                  pl.BlockSpec((1, d), lambda i: (0, 0))],
        out_specs=pl.BlockSpec((tt, d), lambda i: (i, 0)),
        out_shape=jax.ShapeDtypeStruct((m, d), _F32),
        compiler_params=pltpu.CompilerParams(dimension_semantics=("parallel",),
                                             vmem_limit_bytes=_vmem_limit(est)),
        name="final_rmsnorm",
    )(x, g)


def _epilogue(name, acc, x_ref=None, g_ref=None):
    if name == "none":
        return acc
    if name == "gelu":
        return jax.nn.gelu(acc)
    if name == "sigmoid":
        return jax.nn.sigmoid(acc)
    if name == "relu2":
        return jnp.square(jnp.maximum(acc, 0.0))
    if name == "resid":
        return x_ref[...] + g_ref[0] * acc
    raise ValueError(name)


def _mm_kernel(a_ref, w_ref, *rest, epilogue, nk):
    if epilogue == "resid":
        x_ref, g_ref, o_ref = rest[:3]
        scratch = rest[3:]
    else:
        x_ref = g_ref = None
        o_ref = rest[0]
        scratch = rest[1:]
    if nk == 1:
        acc = jnp.dot(a_ref[...], w_ref[...], preferred_element_type=_F32)
        o_ref[...] = _epilogue(epilogue, acc, x_ref, g_ref).astype(o_ref.dtype)
        return
    acc_ref = scratch[0]
    k = pl.program_id(2)

    @pl.when(k == 0)
    def _():
        acc_ref[...] = jnp.zeros_like(acc_ref)

    acc_ref[...] += jnp.dot(a_ref[...], w_ref[...], preferred_element_type=_F32)

    @pl.when(k == nk - 1)
    def _():
        o_ref[...] = _epilogue(epilogue, acc_ref[...], x_ref, g_ref).astype(o_ref.dtype)


def _mm(a, w, *, n_off=0, n_cols=None, epilogue="none", out_dtype=_F32, resid=None, gate=None,
        rows_per_batch=None, tm=1024, tn=1024, tk=None, name="matmul"):
    m, kdim = a.shape
    n_cols = w.shape[1] if n_cols is None else n_cols
    tk = kdim if tk is None else tk
    tm = min(tm, m)
    nk = kdim // tk
    joff = n_off // tn
    grid = (m // tm, n_cols // tn, nk)
    in_specs = [pl.BlockSpec((tm, tk), lambda i, j, k: (i, k)),
                pl.BlockSpec((tk, tn), lambda i, j, k: (k, j + joff))]
    args = [a, w]
    est = 2 * _nbytes((tm, tk), a.dtype) + 2 * _nbytes((tk, tn), w.dtype)
    est += 2 * _nbytes((tm, tn), out_dtype) + 3 * _nbytes((tm, tn), _F32)
    if epilogue == "resid":
        tiles_per_batch = rows_per_batch // tm
        in_specs += [pl.BlockSpec((tm, tn), lambda i, j, k: (i, j)),
                     pl.BlockSpec((1, 1, tn), lambda i, j, k: (i // tiles_per_batch, 0, j))]
        args += [resid, gate]
        est += 2 * _nbytes((tm, tn), _F32)
    scratch = []
    if nk > 1:
        scratch = [pltpu.VMEM((tm, tn), _F32)]
        est += _nbytes((tm, tn), _F32)
    return pl.pallas_call(
        functools.partial(_mm_kernel, epilogue=epilogue, nk=nk),
        grid=grid,
        in_specs=in_specs,
        out_specs=pl.BlockSpec((tm, tn), lambda i, j, k: (i, j)),
        out_shape=jax.ShapeDtypeStruct((m, n_cols), out_dtype),
        scratch_shapes=scratch,
        compiler_params=pltpu.CompilerParams(
            dimension_semantics=("parallel", "parallel", "arbitrary"),
            vmem_limit_bytes=_vmem_limit(est)),
        name=name,
    )(*args)


def _glu_kernel(a_ref, wv_ref, wg_ref, o_ref):
    a = a_ref[...]
    v = jnp.dot(a, wv_ref[...], preferred_element_type=_F32)
    g = jnp.dot(a, wg_ref[...], preferred_element_type=_F32)
    o_ref[...] = (v * jax.nn.sigmoid(g)).astype(o_ref.dtype)


def _glu_mm(a, w, *, v_off, g_off, n_cols, tm=1024, tn=512):
    m, kdim = a.shape
    jv, jg = v_off // tn, g_off // tn
    est = 2 * _nbytes((tm, kdim), a.dtype) + 4 * _nbytes((kdim, tn), w.dtype) + 8 * _nbytes((tm, tn), _F32)
    return pl.pallas_call(
        _glu_kernel,
        grid=(m // tm, n_cols // tn),
        in_specs=[pl.BlockSpec((tm, kdim), lambda i, j: (i, 0)),
                  pl.BlockSpec((kdim, tn), lambda i, j: (0, j + jv)),
                  pl.BlockSpec((kdim, tn), lambda i, j: (0, j + jg))],
        out_specs=pl.BlockSpec((tm, tn), lambda i, j: (i, j)),
        out_shape=jax.ShapeDtypeStruct((m, n_cols), _F32),
        compiler_params=pltpu.CompilerParams(dimension_semantics=("parallel", "parallel"),
                                             vmem_limit_bytes=_vmem_limit(est)),
        name="in_proj_glu",
    )(a, w, w)


def _merge_kernel(m_ref, cf_ref, wl_ref, wc_ref, gl_ref, gc_ref, o_ref, acc_ref, *, nk):
    k = pl.program_id(2)

    @pl.when(k == 0)
    def _():
        acc_ref[...] = jnp.zeros_like(acc_ref)

    lo = jnp.dot(m_ref[...], wl_ref[...], preferred_element_type=_F32)
    co = jnp.dot(cf_ref[...], wc_ref[...], preferred_element_type=_F32)
    acc_ref[...] += gl_ref[...].astype(_F32) * lo + gc_ref[...].astype(_F32) * co

    @pl.when(k == nk - 1)
    def _():
        o_ref[...] = acc_ref[...].astype(o_ref.dtype)


def _merge_mm(m_act, cf_act, w_lru, w_conf, gates, *, tm=1024, tn=1024, tk=2048):
    m, kdim = m_act.shape
    n = w_lru.shape[1]
    nk = kdim // tk
    jc = n // tn
    est = 4 * _nbytes((tm, tk), _BF16) + 4 * _nbytes((tk, tn), _BF16) + 4 * _nbytes((tm, tn), _BF16)
    est += 2 * _nbytes((tm, tn), _BF16) + 4 * _nbytes((tm, tn), _F32)
    return pl.pallas_call(
        functools.partial(_merge_kernel, nk=nk),
        grid=(m // tm, n // tn, nk),
        in_specs=[pl.BlockSpec((tm, tk), lambda i, j, k: (i, k)),
                  pl.BlockSpec((tm, tk), lambda i, j, k: (i, k)),
                  pl.BlockSpec((tk, tn), lambda i, j, k: (k, j)),
                  pl.BlockSpec((tk, tn), lambda i, j, k: (k, j)),
                  pl.BlockSpec((tm, tn), lambda i, j, k: (i, j)),
                  pl.BlockSpec((tm, tn), lambda i, j, k: (i, j + jc))],
        out_specs=pl.BlockSpec((tm, tn), lambda i, j, k: (i, j)),
        out_shape=jax.ShapeDtypeStruct((m, n), _BF16),
        scratch_shapes=[pltpu.VMEM((tm, tn), _F32)],
        compiler_params=pltpu.CompilerParams(
            dimension_semantics=("parallel", "parallel", "arbitrary"),
            vmem_limit_bytes=_vmem_limit(est)),
        name="gated_merge",
    )(m_act, cf_act, w_lru, w_conf, gates, gates)


def _lru_kernel(x_ref, cw_ref, cb_ref, wa_ref, ba_ref, wi_ref, bi_ref, lam_ref, h0_ref, *rest,
                reverse, combine, tt, heads_per_block):
    if combine:
        hf_ref, gate_ref, o_ref, xpad, a_s, b_s, h_s, carry = rest
    else:
        o_ref, xpad, a_s, b_s, carry = rest
        h_s = o_ref
    cb_w = x_ref.shape[1]
    sub = _V7X_SUBLANES
    i = pl.program_id(2)

    halo = slice(tt, tt + sub) if reverse else slice(0, sub)
    body = slice(0, tt) if reverse else slice(sub, sub + tt)

    @pl.when(i == 0)
    def _():
        carry[...] = jnp.broadcast_to(h0_ref[0], carry.shape)
        xpad[halo, :] = jnp.zeros((sub, cb_w), _F32)

    x = x_ref[...]
    xpad[body, :] = x
    y = jnp.broadcast_to(cb_ref[...], (tt, cb_w))
    for k in range(_LRU_CONV):
        off = (_LRU_CONV - 1 - k) if reverse else (sub - (_LRU_CONV - 1) + k)
        y = y + cw_ref[k:k + 1, :] * xpad[pl.ds(off, tt), :]
    xpad[halo, :] = x[0:sub, :] if reverse else x[tt - sub:tt, :]

    lam = lam_ref[...]
    z = -lam
    softplus = jnp.maximum(z, 0.0) + jnp.log1p(jnp.exp(-jnp.abs(z)))
    c8 = -_LRU_C * softplus
    hw = cb_w // heads_per_block
    for hh in range(heads_per_block):
        cs = slice(hh * hw, (hh + 1) * hw)
        yh = y[:, cs]
        yb = yh.astype(_BF16)
        r = jnp.dot(yb, wa_ref[hh], preferred_element_type=_F32) + ba_ref[:, cs]
        g = jnp.dot(yb, wi_ref[hh], preferred_element_type=_F32) + bi_ref[:, cs]
        a = jnp.exp(c8[:, cs] * jax.nn.sigmoid(r))
        a_s[:, cs] = a
        b_s[:, cs] = jnp.sqrt(1.0 - a * a) * jax.nn.sigmoid(g) * yh

    row = lax.broadcasted_iota(jnp.int32, (sub, cb_w), 0)
    nblk = tt // sub

    def blk(j, h):
        jj = (nblk - 1 - j) if reverse else j
        r0 = pl.multiple_of(jj * sub, sub)
        a = a_s[pl.ds(r0, sub), :]
        b = b_s[pl.ds(r0, sub), :]
        for s in (1, 2, 4):
            if reverse:
                keep = row < (sub - s)
                a_sh = jnp.where(keep, pltpu.roll(a, sub - s, 0), 1.0)
                b_sh = jnp.where(keep, pltpu.roll(b, sub - s, 0), 0.0)
            else:
                keep = row >= s
                a_sh = jnp.where(keep, pltpu.roll(a, s, 0), 1.0)
                b_sh = jnp.where(keep, pltpu.roll(b, s, 0), 0.0)
            b = a * b_sh + b
            a = a * a_sh
        hb = a * h + b
        h_s[pl.ds(r0, sub), :] = hb
        last = hb[0:1, :] if reverse else hb[sub - 1:sub, :]
        return jnp.broadcast_to(last, (sub, cb_w))

    carry[...] = lax.fori_loop(0, nblk, blk, carry[...], unroll=2)

    if combine:
        o_ref[...] = ((h_s[...] + hf_ref[...]) * gate_ref[...].astype(_F32)).astype(o_ref.dtype)


def _lru(x2d, cw, cb, wa, ba, wi, bi, lam, h0, *, batch, reverse, hf=None, gate=None,
         tt=512, heads_per_block=2):
    rows, r = x2d.shape
    t = rows // batch
    tt = min(tt, t)
    nt = t // tt
    hw = r // _LRU_HEADS
    cbw = hw * heads_per_block
    combine = hf is not None

    def row_map(b, h, i):
        ti = (nt - 1 - i) if reverse else i
        return (b * nt + ti, h)

    chan = lambda b, h, i: (0, h)
    in_specs = [pl.BlockSpec((tt, cbw), row_map),
                pl.BlockSpec((_LRU_CONV, cbw), chan),
                pl.BlockSpec((1, cbw), chan),
                pl.BlockSpec((heads_per_block, hw, hw), lambda b, h, i: (h, 0, 0)),
                pl.BlockSpec((1, cbw), chan),
                pl.BlockSpec((heads_per_block, hw, hw), lambda b, h, i: (h, 0, 0)),
                pl.BlockSpec((1, cbw), chan),
                pl.BlockSpec((1, cbw), chan),
                pl.BlockSpec((1, 1, cbw), lambda b, h, i: (b, 0, h))]
    args = [x2d, cw, cb, wa, ba, wi, bi, lam, h0]
    scratch = [pltpu.VMEM((tt + _V7X_SUBLANES, cbw), _F32),
               pltpu.VMEM((tt, cbw), _F32),
               pltpu.VMEM((tt, cbw), _F32)]
    if combine:
        in_specs += [pl.BlockSpec((tt, cbw), row_map), pl.BlockSpec((tt, cbw), row_map)]
        args += [hf, gate]
        scratch.append(pltpu.VMEM((tt, cbw), _F32))
    scratch.append(pltpu.VMEM((_V7X_SUBLANES, cbw), _F32))
    tile = _nbytes((tt, cbw), _F32)
    est = 16 * tile + (4 << 20)
    return pl.pallas_call(
        functools.partial(_lru_kernel, reverse=reverse, combine=combine, tt=tt,
                          heads_per_block=heads_per_block),
        grid=(batch, _LRU_HEADS // heads_per_block, nt),
        in_specs=in_specs,
        out_specs=pl.BlockSpec((tt, cbw), row_map),
        out_shape=jax.ShapeDtypeStruct((rows, r), _BF16 if combine else _F32),
        scratch_shapes=scratch,
        compiler_params=pltpu.CompilerParams(
            dimension_semantics=("parallel", "parallel", "arbitrary"),
            vmem_limit_bytes=_vmem_limit(est)),
        name="rglru_bwd" if reverse else "rglru_fwd",
    )(*args)


_COL_HALO = 16


def _conf_conv_kernel(v_ref, w_ref, b_ref, o_ref, pad_col, pad_row, *, cb, n_col_blocks, groups_per_step):
    n_rows, n_cols = v_ref.shape[1], v_ref.shape[2]
    c = pl.program_id(1)

    @pl.when(c < n_col_blocks)
    def _():
        zeros = jnp.zeros((n_rows, _COL_HALO, cb), _F32)
        pad_col[:, 0:_COL_HALO, :] = zeros
        pad_col[:, _COL_HALO:_COL_HALO + n_cols, :] = v_ref[0]
        pad_col[:, _COL_HALO + n_cols:, :] = zeros

        def grp(g, carry):
            acc = jnp.broadcast_to(b_ref[...], (n_cols, cb))
            for k in range(_CONF_K):
                acc = acc + w_ref[k:k + 1, :] * pad_col[g, pl.ds(_COL_HALO - _CONF_PAD + k, n_cols), :]
            o_ref[0, g] = acc
            return carry

        lax.fori_loop(0, n_rows, grp, 0)

    @pl.when(c >= n_col_blocks)
    def _():
        zeros = jnp.zeros((_CONF_PAD, n_cols, cb), _F32)
        pad_row[0:_CONF_PAD] = zeros
        pad_row[_CONF_PAD:_CONF_PAD + n_rows] = v_ref[0]
        pad_row[_CONF_PAD + n_rows:] = zeros

        def chunk(ci, carry):
            g0 = ci * groups_per_step
            acc = jnp.broadcast_to(b_ref[...].reshape(1, 1, cb), (groups_per_step, n_cols, cb))
            for k in range(_CONF_K):
                acc = acc + w_ref[k:k + 1, :].reshape(1, 1, cb) * pad_row[pl.ds(g0 + k, groups_per_step)]
            o_ref[0, pl.ds(g0, groups_per_step)] = acc
            return carry

        lax.fori_loop(0, n_rows // groups_per_step, chunk, 0)


def _conf_conv(v4d, w, b, *, cb=256, groups_per_step=2):
    bsz, n_rows, n_cols, ch = v4d.shape
    n_col_blocks = (ch // 2) // cb
    col_shape = (n_rows, n_cols + 2 * _COL_HALO, cb)
    row_shape = (n_rows + 2 * _CONF_PAD, n_cols, cb)
    est = 4 * _nbytes((n_rows, n_cols, cb), _F32) + _nbytes(col_shape, _F32) + _nbytes(row_shape, _F32)
    return pl.pallas_call(
        functools.partial(_conf_conv_kernel, cb=cb, n_col_blocks=n_col_blocks,
                          groups_per_step=groups_per_step),
        grid=(bsz, ch // cb),
        in_specs=[pl.BlockSpec((1, n_rows, n_cols, cb), lambda i, c: (i, 0, 0, c)),
                  pl.BlockSpec((_CONF_K, cb), lambda i, c: (0, c)),
                  pl.BlockSpec((1, cb), lambda i, c: (0, c))],
        out_specs=pl.BlockSpec((1, n_rows, n_cols, cb), lambda i, c: (i, 0, 0, c)),
        out_shape=jax.ShapeDtypeStruct((bsz, n_rows, n_cols, ch), _F32),
        scratch_shapes=[pltpu.VMEM(col_shape, _F32), pltpu.VMEM(row_shape, _F32)],
        compiler_params=pltpu.CompilerParams(dimension_semantics=("parallel", "parallel"),
                                             vmem_limit_bytes=_vmem_limit(est)),
        name="conformer_dwconv",
    )(v4d, w, b)


def _ln_silu_kernel(x_ref, g_ref, b_ref, o_ref):
    x = x_ref[...]
    mu = jnp.mean(x, axis=-1, keepdims=True)
    xc = x - mu
    y = xc * lax.rsqrt(jnp.mean(xc * xc, axis=-1, keepdims=True) + _EPS)
    y = y * g_ref[...] + b_ref[...]
    o_ref[...] = (y * jax.nn.sigmoid(y)).astype(o_ref.dtype)


def _ln_silu(x, g, b, tt=512):
    m, d = x.shape
    est = 2 * _nbytes((tt, d), _F32) + 2 * _nbytes((tt, d), _BF16) + 3 * _nbytes((tt, d), _F32)
    return pl.pallas_call(
        _ln_silu_kernel,
        grid=(m // tt,),
        in_specs=[pl.BlockSpec((tt, d), lambda i: (i, 0)),
                  pl.BlockSpec((1, d), lambda i: (0, 0)),
                  pl.BlockSpec((1, d), lambda i: (0, 0))],
        out_specs=pl.BlockSpec((tt, d), lambda i: (i, 0)),
        out_shape=jax.ShapeDtypeStruct((m, d), _BF16),
        compiler_params=pltpu.CompilerParams(dimension_semantics=("parallel",),
                                             vmem_limit_bytes=_vmem_limit(est)),
        name="layernorm_swish",
    )(x, g, b)


def kernel(x, c, ctx, c_ctx, w_ada, b_ada, norm1_g, w_in, lru_conv_w, lru_conv_b, lru_w_a, lru_b_a,
           lru_w_i, lru_b_i, lru_lam, w_lru_out, conf_dw_w, conf_dw_b, conf_ln_g, conf_ln_b,
           w_conf_out, w_o, norm2_g, w_ff1, w_ff2, final_g):
    assert w_ada.shape[0] == 1, "single-layer stack only (the context stream is then never updated)"
    bsz, t, d = x.shape
    tc = ctx.shape[1]
    r = lru_lam.shape[-1]
    cw = conf_dw_w.shape[-1]
    row = lambda v: v.reshape(1, -1)

    cc = jnp.concatenate([c, c_ctx[None, :], jnp.zeros((_V7X_SUBLANES - bsz - 1, d), _F32)], axis=0)
    mod = _ada(cc, w_ada[0], row(b_ada[0]))
    mod_l = mod[:bsz].reshape(bsz, 6, 1, d)
    sh1, sc1, g1, sh2, sc2, g2 = (mod_l[:, k] for k in range(6))
    mod_c = jnp.broadcast_to(mod[bsz].reshape(6, 1, 1, d), (6, bsz, 1, d))
    csh1, csc1 = mod_c[0], mod_c[1]

    u = _norm_mod(x, row(norm1_g[0]), sc1, sh1).reshape(bsz * t, d)
    uc = _norm_mod(ctx, row(norm1_g[0]), csc1, csh1).reshape(bsz * tc, d)

    w_in_b = w_in[0].astype(_BF16)
    gate_act = _mm(u, w_in_b, n_off=0, n_cols=r, epilogue="gelu", out_dtype=_BF16, name="in_proj_gate")
    x_br = _mm(u, w_in_b, n_off=r, n_cols=r, epilogue="none", out_dtype=_F32, name="in_proj_lru")
    v = _glu_mm(u, w_in_b, v_off=2 * r, g_off=2 * r + cw, n_cols=cw)
    br_gates = _mm(u, w_in_b, n_off=2 * r + 2 * cw, n_cols=2 * d, epilogue="sigmoid", out_dtype=_BF16,
                   name="in_proj_branch")
    xc_br = _mm(uc, w_in_b, n_off=r, n_cols=r, epilogue="none", out_dtype=_F32, name="in_proj_ctx")

    def lru_params(dirn):
        return (lru_conv_w[0, dirn], row(lru_conv_b[0, dirn]), lru_w_a[0, dirn].astype(_BF16),
                row(lru_b_a[0, dirn]), lru_w_i[0, dirn].astype(_BF16), row(lru_b_i[0, dirn]),
                row(lru_lam[0, dirn]))

    zero_state = jnp.zeros((bsz, 1, r), _F32)
    hc_f = _lru(xc_br, *lru_params(0), zero_state, batch=bsz, reverse=False)
    hc_b = _lru(xc_br, *lru_params(1), zero_state, batch=bsz, reverse=True)
    h0_f = hc_f.reshape(bsz, tc, r)[:, tc - 1:tc, :]
    h0_b = hc_b.reshape(bsz, tc, r)[:, 0:1, :]
    h_f = _lru(x_br, *lru_params(0), h0_f, batch=bsz, reverse=False)
    lru_act = _lru(x_br, *lru_params(1), h0_b, batch=bsz, reverse=True, hf=h_f, gate=gate_act)

    conv = _conf_conv(v.reshape(bsz, t // _GRID_W, _GRID_W, cw), conf_dw_w[0], row(conf_dw_b[0]))
    conf_act = _ln_silu(conv.reshape(bsz * t, cw), row(conf_ln_g[0]), row(conf_ln_b[0]))

    merged = _merge_mm(lru_act, conf_act, w_lru_out[0].astype(_BF16), w_conf_out[0].astype(_BF16), br_gates)
    x1 = _mm(merged, w_o[0].astype(_BF16), epilogue="resid", resid=x.reshape(bsz * t, d), gate=g1,
             rows_per_batch=t, tn=512, name="out_proj_residual")

    u2 = _norm_mod(x1.reshape(bsz, t, d), row(norm2_g[0]), sc2, sh2).reshape(bsz * t, d)
    hid = _mm(u2, w_ff1[0].astype(_BF16), epilogue="relu2", out_dtype=_BF16, name="mlp_up")
    x2 = _mm(hid, w_ff2[0].astype(_BF16), epilogue="resid", resid=x1, gate=g2, rows_per_batch=t,
             tk=2048, name="mlp_down_residual")
    return _final_norm(x2, row(final_g)).reshape(bsz, t, d)
```

```python
import functools

import jax
import jax.numpy as jnp
from jax import lax
from jax.experimental import pallas as pl
from jax.experimental.pallas import tpu as pltpu

_F32 = jnp.float32
_BF16 = jnp.bfloat16

_GRID_W = 64
_LRU_HEADS = 16
_LRU_CONV = 4
_LRU_C = 8.0
_CONF_K = 31
_CONF_PAD = (_CONF_K - 1) // 2
_EPS = 1e-6
_LOG2E = 1.4426950408889634

_V7X_SUBLANES = 8
_V7X_LANES = 128
_V7X_SCOPED_VMEM_BYTES = 60000 * 1024


_COMPILER_SCRATCH_BYTES = 2 << 20


def _vmem_limit(nbytes):
    return int(min(nbytes + _COMPILER_SCRATCH_BYTES, _V7X_SCOPED_VMEM_BYTES))


def _nbytes(shape, dtype):
    n = 1
    for s in shape:
        n *= s
    return n * jnp.dtype(dtype).itemsize


def _ada_kernel(cc_ref, w_ref, b_ref, o_ref):
    cc = cc_ref[...]
    s = cc * jax.nn.sigmoid(cc)
    o_ref[...] = jnp.dot(s.astype(_BF16), w_ref[...].astype(_BF16),
                         preferred_element_type=_F32) + b_ref[...]


def _ada(cc, w, b, tn=512):
    m, d = cc.shape
    n = w.shape[1]
    est = 2 * _nbytes((d, tn), _F32) + _nbytes((d, tn), _BF16) + 4 * _nbytes((m, d), _F32) + (4 << 20)
    return pl.pallas_call(
        _ada_kernel,
        grid=(n // tn,),
        in_specs=[pl.BlockSpec((m, d), lambda j: (0, 0)),
                  pl.BlockSpec((d, tn), lambda j: (0, j)),
                  pl.BlockSpec((1, tn), lambda j: (0, j))],
        out_specs=pl.BlockSpec((m, tn), lambda j: (0, j)),
        out_shape=jax.ShapeDtypeStruct((m, n), _F32),
        compiler_params=pltpu.CompilerParams(dimension_semantics=("parallel",),
                                             vmem_limit_bytes=_vmem_limit(est)),
        name="ada_ln",
    )(cc, w, b)


def _norm_mod_kernel(x_ref, g_ref, sc_ref, sh_ref, o_ref):
    x = x_ref[0]
    ms = jnp.mean(x * x, axis=-1, keepdims=True)
    y = x * lax.rsqrt(ms + _EPS) * g_ref[...]
    o_ref[0] = (y * (1.0 + sc_ref[0]) + sh_ref[0]).astype(o_ref.dtype)


def _norm_mod(x, g, sc, sh, tt=512):
    b, t, d = x.shape
    tt = min(tt, t)
    est = 2 * _nbytes((tt, d), _F32) + 2 * _nbytes((tt, d), _BF16) + 2 * _nbytes((tt, d), _F32)
    return pl.pallas_call(
        _norm_mod_kernel,
        grid=(b, t // tt),
        in_specs=[pl.BlockSpec((1, tt, d), lambda i, j: (i, j, 0)),
                  pl.BlockSpec((1, d), lambda i, j: (0, 0)),
                  pl.BlockSpec((1, 1, d), lambda i, j: (i, 0, 0)),
                  pl.BlockSpec((1, 1, d), lambda i, j: (i, 0, 0))],
        out_specs=pl.BlockSpec((1, tt, d), lambda i, j: (i, j, 0)),
        out_shape=jax.ShapeDtypeStruct((b, t, d), _BF16),
        compiler_params=pltpu.CompilerParams(dimension_semantics=("parallel", "parallel"),
                                             vmem_limit_bytes=_vmem_limit(est)),
        name="rmsnorm_modulate",
    )(x, g, sc, sh)


def _final_norm_kernel(x_ref, g_ref, o_ref):
    x = x_ref[...]
    ms = jnp.mean(x * x, axis=-1, keepdims=True)
    o_ref[...] = x * lax.rsqrt(ms + _EPS) * g_ref[...]


def _final_norm(x, g, tt=512):
    m, d = x.shape
    est = 4 * _nbytes((tt, d), _F32) + 2 * _nbytes((tt, d), _F32)
    return pl.pallas_call(
        _final_norm_kernel,
        grid=(m // tt,),
        in_specs=[pl.BlockSpec((tt, d), lambda i: (i, 0)),
                  pl.BlockSpec((1, d), lambda i: (0, 0))],
        out_specs=pl.BlockSpec((tt, d), lambda i: (i, 0)),
        out_shape=jax.ShapeDtypeStruct((m, d), _F32),
        compiler_params=pltpu.CompilerParams(dimension_semantics=("parallel",),
                                             vmem_limit_bytes=_vmem_limit(est)),
        name="final_rmsnorm",
    )(x, g)


def _epilogue(name, acc, x_ref=None, g_ref=None):
    if name == "none":
        return acc
    if name == "gelu":
        return jax.nn.gelu(acc)
    if name == "sigmoid":
        return jax.nn.sigmoid(acc)
    if name == "relu2":
        return jnp.square(jnp.maximum(acc, 0.0))
    if name == "resid":
        return x_ref[...] + g_ref[0] * acc
    raise ValueError(name)


def _mm_kernel(a_ref, w_ref, *rest, epilogue, nk):
    if epilogue == "resid":
        x_ref, g_ref, o_ref = rest[:3]
        scratch = rest[3:]
    else:
        x_ref = g_ref = None
        o_ref = rest[0]
        scratch = rest[1:]
    if nk == 1:
        acc = jnp.dot(a_ref[...], w_ref[...], preferred_element_type=_F32)
        o_ref[...] = _epilogue(epilogue, acc, x_ref, g_ref).astype(o_ref.dtype)
        return
    acc_ref = scratch[0]
    k = pl.program_id(2)

    @pl.when(k == 0)
    def _():
        acc_ref[...] = jnp.zeros_like(acc_ref)

    acc_ref[...] += jnp.dot(a_ref[...], w_ref[...], preferred_element_type=_F32)

    @pl.when(k == nk - 1)
    def _():
        o_ref[...] = _epilogue(epilogue, acc_ref[...], x_ref, g_ref).astype(o_ref.dtype)


def _mm(a, w, *, n_off=0, n_cols=None, epilogue="none", out_dtype=_F32, resid=None, gate=None,
        rows_per_batch=None, tm=1024, tn=1024, tk=None, name="matmul"):
    m, kdim = a.shape
    n_cols = w.shape[1] if n_cols is None else n_cols
    tk = kdim if tk is None else tk
    tm = min(tm, m)
    nk = kdim // tk
    joff = n_off // tn
    grid = (m // tm, n_cols // tn, nk)
    in_specs = [pl.BlockSpec((tm, tk), lambda i, j, k: (i, k)),
                pl.BlockSpec((tk, tn), lambda i, j, k: (k, j + joff))]
    args = [a, w]
    est = 2 * _nbytes((tm, tk), a.dtype) + 2 * _nbytes((tk, tn), w.dtype)
    est += 2 * _nbytes((tm, tn), out_dtype) + 3 * _nbytes((tm, tn), _F32)
    if epilogue == "resid":
        tiles_per_batch = rows_per_batch // tm
        in_specs += [pl.BlockSpec((tm, tn), lambda i, j, k: (i, j)),
                     pl.BlockSpec((1, 1, tn), lambda i, j, k: (i // tiles_per_batch, 0, j))]
        args += [resid, gate]
        est += 2 * _nbytes((tm, tn), _F32)
    scratch = []
    if nk > 1:
        scratch = [pltpu.VMEM((tm, tn), _F32)]
        est += _nbytes((tm, tn), _F32)
    return pl.pallas_call(
        functools.partial(_mm_kernel, epilogue=epilogue, nk=nk),
        grid=grid,
        in_specs=in_specs,
        out_specs=pl.BlockSpec((tm, tn), lambda i, j, k: (i, j)),
        out_shape=jax.ShapeDtypeStruct((m, n_cols), out_dtype),
        scratch_shapes=scratch,
        compiler_params=pltpu.CompilerParams(
            dimension_semantics=("parallel", "parallel", "arbitrary"),
            vmem_limit_bytes=_vmem_limit(est)),
        name=name,
    )(*args)


def _glu_kernel(a_ref, wv_ref, wg_ref, o_ref):
    a = a_ref[...]
    v = jnp.dot(a, wv_ref[...], preferred_element_type=_F32)
    g = jnp.dot(a, wg_ref[...], preferred_element_type=_F32)
    o_ref[...] = (v * jax.nn.sigmoid(g)).astype(o_ref.dtype)


def _glu_mm(a, w, *, v_off, g_off, n_cols, tm=1024, tn=512):
    m, kdim = a.shape
    jv, jg = v_off // tn, g_off // tn
    est = 2 * _nbytes((tm, kdim), a.dtype) + 4 * _nbytes((kdim, tn), w.dtype) + 8 * _nbytes((tm, tn), _F32)
    return pl.pallas_call(
        _glu_kernel,
        grid=(m // tm, n_cols // tn),
        in_specs=[pl.BlockSpec((tm, kdim), lambda i, j: (i, 0)),
                  pl.BlockSpec((kdim, tn), lambda i, j: (0, j + jv)),
                  pl.BlockSpec((kdim, tn), lambda i, j: (0, j + jg))],
        out_specs=pl.BlockSpec((tm, tn), lambda i, j: (i, j)),
        out_shape=jax.ShapeDtypeStruct((m, n_cols), _F32),
        compiler_params=pltpu.CompilerParams(dimension_semantics=("parallel", "parallel"),
                                             vmem_limit_bytes=_vmem_limit(est)),
        name="in_proj_glu",
    )(a, w, w)


def _merge_kernel(m_ref, cf_ref, wl_ref, wc_ref, gl_ref, gc_ref, o_ref, acc_ref, *, nk):
    k = pl.program_id(2)

    @pl.when(k == 0)
    def _():
        acc_ref[...] = jnp.zeros_like(acc_ref)

    lo = jnp.dot(m_ref[...], wl_ref[...], preferred_element_type=_F32)
    co = jnp.dot(cf_ref[...], wc_ref[...], preferred_element_type=_F32)
    acc_ref[...] += gl_ref[...].astype(_F32) * lo + gc_ref[...].astype(_F32) * co

    @pl.when(k == nk - 1)
    def _():
        o_ref[...] = acc_ref[...].astype(o_ref.dtype)


def _merge_mm(m_act, cf_act, w_lru, w_conf, gates, *, tm=1024, tn=1024, tk=2048):
    m, kdim = m_act.shape
    n = w_lru.shape[1]
    nk = kdim // tk
    jc = n // tn
    est = 4 * _nbytes((tm, tk), _BF16) + 4 * _nbytes((tk, tn), _BF16) + 4 * _nbytes((tm, tn), _BF16)
    est += 2 * _nbytes((tm, tn), _BF16) + 4 * _nbytes((tm, tn), _F32)
    return pl.pallas_call(
        functools.partial(_merge_kernel, nk=nk),
        grid=(m // tm, n // tn, nk),
        in_specs=[pl.BlockSpec((tm, tk), lambda i, j, k: (i, k)),
                  pl.BlockSpec((tm, tk), lambda i, j, k: (i, k)),
                  pl.BlockSpec((tk, tn), lambda i, j, k: (k, j)),
                  pl.BlockSpec((tk, tn), lambda i, j, k: (k, j)),
                  pl.BlockSpec((tm, tn), lambda i, j, k: (i, j)),
                  pl.BlockSpec((tm, tn), lambda i, j, k: (i, j + jc))],
        out_specs=pl.BlockSpec((tm, tn), lambda i, j, k: (i, j)),
        out_shape=jax.ShapeDtypeStruct((m, n), _BF16),
        scratch_shapes=[pltpu.VMEM((tm, tn), _F32)],
        compiler_params=pltpu.CompilerParams(
            dimension_semantics=("parallel", "parallel", "arbitrary"),
            vmem_limit_bytes=_vmem_limit(est)),
        name="gated_merge",
    )(m_act, cf_act, w_lru, w_conf, gates, gates)


def _lru_kernel(x_ref, cw_ref, cb_ref, wa_ref, ba_ref, wi_ref, bi_ref, lam_ref, h0_ref, *rest,
                reverse, combine, tt, heads_per_block):
    if combine:
        hf_ref, gate_ref, o_ref, px, xcp, xedge, carry = rest
    else:
        o_ref, px, xcp, xedge, carry = rest
    cbw = x_ref.shape[1]
    slabs = cbw // _V7X_LANES
    sub = _V7X_SUBLANES
    ln = tt // sub
    pitch = ln + sub
    nhalo = _LRU_CONV - 1
    base = 0 if reverse else nhalo * sub
    i = pl.program_id(2)
    srow = lax.broadcasted_iota(jnp.int32, (sub, _V7X_LANES), 0)

    @pl.when(i == 0)
    def _():
        carry[...] = h0_ref[0]
        xedge[...] = jnp.zeros_like(xedge)

    for s in range(slabs):
        lanes = slice(s * _V7X_LANES, (s + 1) * _V7X_LANES)
        for c in range(sub):
            px[s, c * pitch:c * pitch + ln, :] = x_ref[c * ln:(c + 1) * ln, lanes]
        for j in range(ln):
            xcp[s, base + j * sub:base + (j + 1) * sub, :] = px[s, pl.ds(j, sub, stride=pitch), :]
        for m in range(nhalo):
            prev = xedge[s, m * sub:(m + 1) * sub, :]
            if reverse:
                cur = xcp[s, m * sub:(m + 1) * sub, :]
                xedge[s, m * sub:(m + 1) * sub, :] = cur
                xcp[s, (ln + m) * sub:(ln + m + 1) * sub, :] = pltpu.roll(jnp.where(srow == 0, prev, cur), sub - 1, 0)
            else:
                cur = xcp[s, base + (ln - nhalo + m) * sub:base + (ln - nhalo + m + 1) * sub, :]
                xedge[s, m * sub:(m + 1) * sub, :] = cur
                xcp[s, m * sub:(m + 1) * sub, :] = pltpu.roll(jnp.where(srow == sub - 1, prev, cur), 1, 0)

    def conv_slab(s):
        lanes = slice(s * _V7X_LANES, (s + 1) * _V7X_LANES)
        y = jnp.broadcast_to(cb_ref[:, lanes], (tt, _V7X_LANES))
        for k in range(_LRU_CONV):
            off = (nhalo - k) if reverse else k
            y = y + cw_ref[k:k + 1, lanes] * xcp[s, off * sub:off * sub + tt, :]
        return y

    lam = lam_ref[...]
    z = -lam
    softplus = jnp.maximum(z, 0.0) + jnp.log1p(jnp.exp(-jnp.abs(z)))
    c8 = (-_LRU_C * _LOG2E) * softplus
    hw = cbw // heads_per_block
    order = range(ln - 1, -1, -1) if reverse else range(ln)
    chunk_order = range(sub - 1, -1, -1) if reverse else range(sub)
    for hh in range(heads_per_block):
        cs = slice(hh * hw, (hh + 1) * hw)
        head_slabs = range(hh * hw // _V7X_LANES, (hh + 1) * hw // _V7X_LANES)
        yh = jnp.concatenate([conv_slab(s) for s in head_slabs], axis=1)
        yb = yh.astype(_BF16)
        r = jnp.dot(yb, wa_ref[hh], preferred_element_type=_F32) + ba_ref[:, cs]
        g = jnp.dot(yb, wi_ref[hh], preferred_element_type=_F32) + bi_ref[:, cs]
        a = jnp.exp2(c8[:, cs] * jax.nn.sigmoid(r))
        v = 1.0 - a * a
        b = jnp.where(v > 0.0, v * lax.rsqrt(v), 0.0) * jax.nn.sigmoid(g) * yh

        h = jnp.zeros((sub, hw), _F32)
        p = jnp.ones((sub, hw), _F32)
        hs, ps = [None] * ln, [None] * ln
        for j in order:
            aj = a[j * sub:(j + 1) * sub]
            h = aj * h + b[j * sub:(j + 1) * sub]
            p = aj * p
            hs[j], ps[j] = h, p
        init = carry[:, cs]
        inits = [None] * sub
        for c in chunk_order:
            inits[c] = init
            init = p[c:c + 1, :] * init + h[c:c + 1, :]
        carry[:, cs] = init
        init_all = jnp.concatenate(inits, axis=0)
        for j in range(ln):
            hj = ps[j] * init_all + hs[j]
            for n, s in enumerate(head_slabs):
                px[s, pl.ds(j, sub, stride=pitch), :] = hj[:, n * _V7X_LANES:(n + 1) * _V7X_LANES]
        for s in head_slabs:
            lanes = slice(s * _V7X_LANES, (s + 1) * _V7X_LANES)
            for c in range(sub):
                rows = slice(c * ln, (c + 1) * ln)
                hn = px[s, c * pitch:c * pitch + ln, :]
                if combine:
                    hn = (hn + hf_ref[rows, lanes]) * gate_ref[rows, lanes].astype(_F32)
                o_ref[rows, lanes] = hn.astype(o_ref.dtype)


def _lru(x2d, cw, cb, wa, ba, wi, bi, lam, h0, *, batch, reverse, hf=None, gate=None,
         tt=512, heads_per_block=2):
    rows, r = x2d.shape
    t = rows // batch
    tt = min(tt, t)
    nt = t // tt
    hw = r // _LRU_HEADS
    cbw = hw * heads_per_block
    slabs = cbw // _V7X_LANES
    ln = tt // _V7X_SUBLANES
    combine = hf is not None

    def row_map(b, h, i):
        ti = (nt - 1 - i) if reverse else i
        return (b * nt + ti, h)

    chan = lambda b, h, i: (0, h)
    in_specs = [pl.BlockSpec((tt, cbw), row_map),
                pl.BlockSpec((_LRU_CONV, cbw), chan),
                pl.BlockSpec((1, cbw), chan),
                pl.BlockSpec((heads_per_block, hw, hw), lambda b, h, i: (h, 0, 0)),
                pl.BlockSpec((1, cbw), chan),
                pl.BlockSpec((heads_per_block, hw, hw), lambda b, h, i: (h, 0, 0)),
                pl.BlockSpec((1, cbw), chan),
                pl.BlockSpec((1, cbw), chan),
                pl.BlockSpec((1, 1, cbw), lambda b, h, i: (b, 0, h))]
    args = [x2d, cw, cb, wa, ba, wi, bi, lam, h0]
    if combine:
        in_specs += [pl.BlockSpec((tt, cbw), row_map), pl.BlockSpec((tt, cbw), row_map)]
        args += [hf, gate]
    px_shape = (slabs, _V7X_SUBLANES * (ln + _V7X_SUBLANES), _V7X_LANES)
    xcp_shape = (slabs, (ln + _LRU_CONV - 1) * _V7X_SUBLANES, _V7X_LANES)
    edge_shape = (slabs, (_LRU_CONV - 1) * _V7X_SUBLANES, _V7X_LANES)
    scratch = [pltpu.VMEM(px_shape, _F32), pltpu.VMEM(xcp_shape, _F32), pltpu.VMEM(edge_shape, _F32),
               pltpu.VMEM((1, cbw), _F32)]
    tile = _nbytes((tt, cbw), _F32)
    est = 8 * tile + _nbytes(px_shape, _F32) + _nbytes(xcp_shape, _F32) + 8 * tile
    return pl.pallas_call(
        functools.partial(_lru_kernel, reverse=reverse, combine=combine, tt=tt,
                          heads_per_block=heads_per_block),
        grid=(batch, _LRU_HEADS // heads_per_block, nt),
        in_specs=in_specs,
        out_specs=pl.BlockSpec((tt, cbw), row_map),
        out_shape=jax.ShapeDtypeStruct((rows, r), _BF16 if combine else _F32),
        scratch_shapes=scratch,
        compiler_params=pltpu.CompilerParams(
            dimension_semantics=("parallel", "parallel", "arbitrary"),
            vmem_limit_bytes=_vmem_limit(est)),
        name="rglru_bwd" if reverse else "rglru_fwd",
    )(*args)


def _conf_conv_kernel(v_ref, w_ref, b_ref, o_ref, z, pt, *, n, cb, n_col_blocks, lines_per_step):
    slabs = cb // _V7X_LANES
    pitch = n + _V7X_SUBLANES
    c = pl.program_id(1)
    zeros = jnp.zeros((_CONF_PAD * n, _V7X_LANES), _F32)

    def conv(s, write_line_block):
        def step(jj, carry):
            l0 = jj * lines_per_step
            accs = [jnp.broadcast_to(b_ref[:, s * _V7X_LANES:(s + 1) * _V7X_LANES], (n, _V7X_LANES))] * lines_per_step
            for m in range(lines_per_step + _CONF_K - 1):
                line = z[s, pl.ds(pl.multiple_of((l0 + m) * n, n), n), :]
                for l in range(lines_per_step):
                    k = m - l
                    if 0 <= k < _CONF_K:
                        accs[l] = accs[l] + w_ref[k:k + 1, s * _V7X_LANES:(s + 1) * _V7X_LANES] * line
            write_line_block(l0, accs)
            return carry
        lax.fori_loop(0, n // lines_per_step, step, 0)

    for s in range(slabs):
        lanes = slice(s * _V7X_LANES, (s + 1) * _V7X_LANES)
        z[s, 0:_CONF_PAD * n, :] = zeros
        z[s, (_CONF_PAD + n) * n:, :] = zeros

        @pl.when(c < n_col_blocks)
        def _():
            for i in range(n):
                pt[s, i * pitch:i * pitch + n, :] = v_ref[0, i * n:(i + 1) * n, lanes]
            for j in range(n):
                for i0 in range(n // _V7X_SUBLANES):
                    r0 = (_CONF_PAD + j) * n + i0 * _V7X_SUBLANES
                    z[s, r0:r0 + _V7X_SUBLANES, :] = pt[
                        s, pl.ds(i0 * _V7X_SUBLANES * pitch + j, _V7X_SUBLANES, stride=pitch), :]

            def write(l0, accs):
                for l in range(lines_per_step):
                    pt[s, pl.ds(pl.multiple_of((l0 + l) * pitch, _V7X_SUBLANES), n), :] = accs[l]
            conv(s, write)

            for i in range(n):
                for j0 in range(n // _V7X_SUBLANES):
                    r0 = i * n + j0 * _V7X_SUBLANES
                    o_ref[0, r0:r0 + _V7X_SUBLANES, lanes] = pt[
                        s, pl.ds(j0 * _V7X_SUBLANES * pitch + i, _V7X_SUBLANES, stride=pitch), :]

        @pl.when(c >= n_col_blocks)
        def _():
            z[s, _CONF_PAD * n:(_CONF_PAD + n) * n, :] = v_ref[0, :, lanes]

            def write(l0, accs):
                for l in range(lines_per_step):
                    o_ref[0, pl.ds(pl.multiple_of((l0 + l) * n, n), n), lanes] = accs[l]
            conv(s, write)


def _conf_conv(v3d, w, b, *, n, cb=256, lines_per_step=4):
    bsz, t, ch = v3d.shape
    assert t == n * n and (ch // 2) % cb == 0
    n_col_blocks = (ch // 2) // cb
    slabs = cb // _V7X_LANES
    z_shape = (slabs, (n + 2 * _CONF_PAD) * n, _V7X_LANES)
    pt_shape = (slabs, n * (n + _V7X_SUBLANES), _V7X_LANES)
    est = 4 * _nbytes((t, cb), _F32) + _nbytes(z_shape, _F32) + _nbytes(pt_shape, _F32)
    return pl.pallas_call(
        functools.partial(_conf_conv_kernel, n=n, cb=cb, n_col_blocks=n_col_blocks,
                          lines_per_step=lines_per_step),
        grid=(bsz, ch // cb),
        in_specs=[pl.BlockSpec((1, t, cb), lambda i, c: (i, 0, c)),
                  pl.BlockSpec((_CONF_K, cb), lambda i, c: (0, c)),
                  pl.BlockSpec((1, cb), lambda i, c: (0, c))],
        out_specs=pl.BlockSpec((1, t, cb), lambda i, c: (i, 0, c)),
        out_shape=jax.ShapeDtypeStruct((bsz, t, ch), _F32),
        scratch_shapes=[pltpu.VMEM(z_shape, _F32), pltpu.VMEM(pt_shape, _F32)],
        compiler_params=pltpu.CompilerParams(dimension_semantics=("parallel", "parallel"),
                                             vmem_limit_bytes=_vmem_limit(est)),
        name="conformer_dwconv",
    )(v3d, w, b)


def _ln_silu_kernel(x_ref, g_ref, b_ref, o_ref):
    x = x_ref[...]
    mu = jnp.mean(x, axis=-1, keepdims=True)
    xc = x - mu
    y = xc * lax.rsqrt(jnp.mean(xc * xc, axis=-1, keepdims=True) + _EPS)
    y = y * g_ref[...] + b_ref[...]
    o_ref[...] = (y * jax.nn.sigmoid(y)).astype(o_ref.dtype)


def _ln_silu(x, g, b, tt=512):
    m, d = x.shape
    est = 2 * _nbytes((tt, d), _F32) + 2 * _nbytes((tt, d), _BF16) + 3 * _nbytes((tt, d), _F32)
    return pl.pallas_call(
        _ln_silu_kernel,
        grid=(m // tt,),
        in_specs=[pl.BlockSpec((tt, d), lambda i: (i, 0)),
                  pl.BlockSpec((1, d), lambda i: (0, 0)),
                  pl.BlockSpec((1, d), lambda i: (0, 0))],
        out_specs=pl.BlockSpec((tt, d), lambda i: (i, 0)),
        out_shape=jax.ShapeDtypeStruct((m, d), _BF16),
        compiler_params=pltpu.CompilerParams(dimension_semantics=("parallel",),
                                             vmem_limit_bytes=_vmem_limit(est)),
        name="layernorm_swish",
    )(x, g, b)


def kernel(x, c, ctx, c_ctx, w_ada, b_ada, norm1_g, w_in, lru_conv_w, lru_conv_b, lru_w_a, lru_b_a,
           lru_w_i, lru_b_i, lru_lam, w_lru_out, conf_dw_w, conf_dw_b, conf_ln_g, conf_ln_b,
           w_conf_out, w_o, norm2_g, w_ff1, w_ff2, final_g):
    assert w_ada.shape[0] == 1, "single-layer stack only (the context stream is then never updated)"
    bsz, t, d = x.shape
    tc = ctx.shape[1]
    r = lru_lam.shape[-1]
    cw = conf_dw_w.shape[-1]
    row = lambda v: v.reshape(1, -1)

    cc = jnp.concatenate([c, c_ctx[None, :], jnp.zeros((_V7X_SUBLANES - bsz - 1, d), _F32)], axis=0)
    mod = _ada(cc, w_ada[0], row(b_ada[0]))
    mod_l = mod[:bsz].reshape(bsz, 6, 1, d)
    sh1, sc1, g1, sh2, sc2, g2 = (mod_l[:, k] for k in range(6))
    mod_c = jnp.broadcast_to(mod[bsz].reshape(6, 1, 1, d), (6, bsz, 1, d))
    csh1, csc1 = mod_c[0], mod_c[1]

    u = _norm_mod(x, row(norm1_g[0]), sc1, sh1).reshape(bsz * t, d)
    uc = _norm_mod(ctx, row(norm1_g[0]), csc1, csh1).reshape(bsz * tc, d)

    w_in_b = w_in[0].astype(_BF16)
    gate_act = _mm(u, w_in_b, n_off=0, n_cols=r, epilogue="gelu", out_dtype=_BF16, name="in_proj_gate")
    x_br = _mm(u, w_in_b, n_off=r, n_cols=r, epilogue="none", out_dtype=_F32, name="in_proj_lru")
    v = _glu_mm(u, w_in_b, v_off=2 * r, g_off=2 * r + cw, n_cols=cw)
    br_gates = _mm(u, w_in_b, n_off=2 * r + 2 * cw, n_cols=2 * d, epilogue="sigmoid", out_dtype=_BF16,
                   name="in_proj_branch")
    xc_br = _mm(uc, w_in_b, n_off=r, n_cols=r, epilogue="none", out_dtype=_F32, name="in_proj_ctx")

    def lru_params(dirn):
        return (lru_conv_w[0, dirn], row(lru_conv_b[0, dirn]), lru_w_a[0, dirn].astype(_BF16),
                row(lru_b_a[0, dirn]), lru_w_i[0, dirn].astype(_BF16), row(lru_b_i[0, dirn]),
                row(lru_lam[0, dirn]))

    zero_state = jnp.zeros((bsz, 1, r), _F32)
    hc_f = _lru(xc_br, *lru_params(0), zero_state, batch=bsz, reverse=False)
    hc_b = _lru(xc_br, *lru_params(1), zero_state, batch=bsz, reverse=True)
    h0_f = hc_f.reshape(bsz, tc, r)[:, tc - 1:tc, :]
    h0_b = hc_b.reshape(bsz, tc, r)[:, 0:1, :]
    h_f = _lru(x_br, *lru_params(0), h0_f, batch=bsz, reverse=False)
    lru_act = _lru(x_br, *lru_params(1), h0_b, batch=bsz, reverse=True, hf=h_f, gate=gate_act)

    conv = _conf_conv(v.reshape(bsz, t, cw), conf_dw_w[0], row(conf_dw_b[0]), n=_GRID_W)
    conf_act = _ln_silu(conv.reshape(bsz * t, cw), row(conf_ln_g[0]), row(conf_ln_b[0]))

    merged = _merge_mm(lru_act, conf_act, w_lru_out[0].astype(_BF16), w_conf_out[0].astype(_BF16), br_gates)
    x1 = _mm(merged, w_o[0].astype(_BF16), epilogue="resid", resid=x.reshape(bsz * t, d), gate=g1,
             rows_per_batch=t, tn=512, name="out_proj_residual")

    u2 = _norm_mod(x1.reshape(bsz, t, d), row(norm2_g[0]), sc2, sh2).reshape(bsz * t, d)
    hid = _mm(u2, w_ff1[0].astype(_BF16), epilogue="relu2", out_dtype=_BF16, name="mlp_up")
    x2 = _mm(hid, w_ff2[0].astype(_BF16), epilogue="resid", resid=x1, gate=g2, rows_per_batch=t,
             tk=2048, name="mlp_down_residual")
    return _final_norm(x2, row(final_g)).reshape(bsz, t, d)
```

```python
import functools

import jax
import jax.numpy as jnp
from jax import lax
from jax.experimental import pallas as pl
from jax.experimental.pallas import tpu as pltpu

_F32 = jnp.float32
_BF16 = jnp.bfloat16

_GRID_W = 64
_LRU_HEADS = 16
_LRU_CONV = 4
_LRU_C = 8.0
_CONF_K = 31
_CONF_PAD = (_CONF_K - 1) // 2
_EPS = 1e-6
_LOG2E = 1.4426950408889634

_V7X_SUBLANES = 8
_V7X_LANES = 128
_V7X_SCOPED_VMEM_BYTES = 60000 * 1024


_COMPILER_SCRATCH_BYTES = 2 << 20


def _vmem_limit(nbytes):
    return int(min(nbytes + _COMPILER_SCRATCH_BYTES, _V7X_SCOPED_VMEM_BYTES))


def _nbytes(shape, dtype):
    n = 1
    for s in shape:
        n *= s
    return n * jnp.dtype(dtype).itemsize


def _sigmoid(x):
    return 0.5 * jnp.tanh(0.5 * x) + 0.5


def _ada_kernel(cc_ref, w_ref, b_ref, o_ref):
    cc = cc_ref[...]
    s = cc * _sigmoid(cc)
    o_ref[...] = jnp.dot(s.astype(_BF16), w_ref[...].astype(_BF16),
                         preferred_element_type=_F32) + b_ref[...]


def _ada(cc, w, b, tn=512):
    m, d = cc.shape
    n = w.shape[1]
    est = 2 * _nbytes((d, tn), _F32) + _nbytes((d, tn), _BF16) + 4 * _nbytes((m, d), _F32) + (4 << 20)
    return pl.pallas_call(
        _ada_kernel,
        grid=(n // tn,),
        in_specs=[pl.BlockSpec((m, d), lambda j: (0, 0)),
                  pl.BlockSpec((d, tn), lambda j: (0, j)),
                  pl.BlockSpec((1, tn), lambda j: (0, j))],
        out_specs=pl.BlockSpec((m, tn), lambda j: (0, j)),
        out_shape=jax.ShapeDtypeStruct((m, n), _F32),
        compiler_params=pltpu.CompilerParams(dimension_semantics=("parallel",),
                                             vmem_limit_bytes=_vmem_limit(est)),
        name="ada_ln",
    )(cc, w, b)


def _norm_mod_kernel(x_ref, g_ref, sc_ref, sh_ref, o_ref):
    x = x_ref[0]
    ms = jnp.mean(x * x, axis=-1, keepdims=True)
    y = x * lax.rsqrt(ms + _EPS) * g_ref[...]
    o_ref[0] = (y * (1.0 + sc_ref[0]) + sh_ref[0]).astype(o_ref.dtype)


def _norm_mod(x, g, sc, sh, tt=512):
    b, t, d = x.shape
    tt = min(tt, t)
    est = 2 * _nbytes((tt, d), _F32) + 2 * _nbytes((tt, d), _BF16) + 2 * _nbytes((tt, d), _F32)
    return pl.pallas_call(
        _norm_mod_kernel,
        grid=(b, t // tt),
        in_specs=[pl.BlockSpec((1, tt, d), lambda i, j: (i, j, 0)),
                  pl.BlockSpec((1, d), lambda i, j: (0, 0)),
                  pl.BlockSpec((1, 1, d), lambda i, j: (i, 0, 0)),
                  pl.BlockSpec((1, 1, d), lambda i, j: (i, 0, 0))],
        out_specs=pl.BlockSpec((1, tt, d), lambda i, j: (i, j, 0)),
        out_shape=jax.ShapeDtypeStruct((b, t, d), _BF16),
        compiler_params=pltpu.CompilerParams(dimension_semantics=("parallel", "parallel"),
                                             vmem_limit_bytes=_vmem_limit(est)),
        name="rmsnorm_modulate",
    )(x, g, sc, sh)


def _final_norm_kernel(x_ref, g_ref, o_ref):
    x = x_ref[...]
    ms = jnp.mean(x * x, axis=-1, keepdims=True)
    o_ref[...] = x * lax.rsqrt(ms + _EPS) * g_ref[...]


def _final_norm(x, g, tt=512):
    m, d = x.shape
    est = 4 * _nbytes((tt, d), _F32) + 2 * _nbytes((tt, d), _F32)
    return pl.pallas_call(
        _final_norm_kernel,
        grid=(m // tt,),
        in_specs=[pl.BlockSpec((tt, d), lambda i: (i, 0)),
                  pl.BlockSpec((1, d), lambda i: (0, 0))],
        out_specs=pl.BlockSpec((tt, d), lambda i: (i, 0)),
        out_shape=jax.ShapeDtypeStruct((m, d), _F32),
        compiler_params=pltpu.CompilerParams(dimension_semantics=("parallel",),
                                             vmem_limit_bytes=_vmem_limit(est)),
        name="final_rmsnorm",
    )(x, g)


def _epilogue(name, acc, x_ref=None, g_ref=None):
    if name == "none":
        return acc
    if name == "gelu":
        return jax.nn.gelu(acc)
    if name == "sigmoid":
        return _sigmoid(acc)
    if name == "relu2":
        return jnp.square(jnp.maximum(acc, 0.0))
    if name == "resid":
        return x_ref[...] + g_ref[0] * acc
    raise ValueError(name)


def _mm_kernel(a_ref, w_ref, *rest, epilogue, nk):
    if epilogue == "resid":
        x_ref, g_ref, o_ref = rest[:3]
        scratch = rest[3:]
    else:
        x_ref = g_ref = None
        o_ref = rest[0]
        scratch = rest[1:]
    if nk == 1:
        acc = jnp.dot(a_ref[...], w_ref[...], preferred_element_type=_F32)
        o_ref[...] = _epilogue(epilogue, acc, x_ref, g_ref).astype(o_ref.dtype)
        return
    acc_ref = scratch[0] if scratch else o_ref
    k = pl.program_id(2)

    @pl.when(k == 0)
    def _():
        acc_ref[...] = jnp.zeros_like(acc_ref)

    acc_ref[...] += jnp.dot(a_ref[...], w_ref[...], preferred_element_type=_F32)

    @pl.when(k == nk - 1)
    def _():
        o_ref[...] = _epilogue(epilogue, acc_ref[...], x_ref, g_ref).astype(o_ref.dtype)


def _mm(a, w, *, n_off=0, n_cols=None, epilogue="none", out_dtype=_F32, resid=None, gate=None,
        rows_per_batch=None, tm=1024, tn=1024, tk=None, name="matmul"):
    m, kdim = a.shape
    n_cols = w.shape[1] if n_cols is None else n_cols
    tk = kdim if tk is None else tk
    tm = min(tm, m)
    nk = kdim // tk
    joff = n_off // tn
    grid = (m // tm, n_cols // tn, nk)
    in_specs = [pl.BlockSpec((tm, tk), lambda i, j, k: (i, k)),
                pl.BlockSpec((tk, tn), lambda i, j, k: (k, j + joff))]
    args = [a, w]
    est = 2 * _nbytes((tm, tk), a.dtype) + 2 * _nbytes((tk, tn), w.dtype)
    est += 2 * _nbytes((tm, tn), out_dtype) + 3 * _nbytes((tm, tn), _F32)
    if epilogue == "resid":
        tiles_per_batch = rows_per_batch // tm
        in_specs += [pl.BlockSpec((tm, tn), lambda i, j, k: (i, j)),
                     pl.BlockSpec((1, 1, tn), lambda i, j, k: (i // tiles_per_batch, 0, j))]
        args += [resid, gate]
        est += 2 * _nbytes((tm, tn), _F32)
    scratch = []
    if nk > 1 and out_dtype != _F32:
        scratch = [pltpu.VMEM((tm, tn), _F32)]
        est += _nbytes((tm, tn), _F32)
    return pl.pallas_call(
        functools.partial(_mm_kernel, epilogue=epilogue, nk=nk),
        grid=grid,
        in_specs=in_specs,
        out_specs=pl.BlockSpec((tm, tn), lambda i, j, k: (i, j)),
        out_shape=jax.ShapeDtypeStruct((m, n_cols), out_dtype),
        scratch_shapes=scratch,
        compiler_params=pltpu.CompilerParams(
            dimension_semantics=("parallel", "parallel", "arbitrary"),
            vmem_limit_bytes=_vmem_limit(est)),
        name=name,
    )(*args)


def _proj_kernel(a_ref, *rest, epilogue, n_w):
    w_refs, o_ref, wb_refs = rest[:n_w], rest[n_w], rest[n_w + 1:]

    @pl.when(pl.program_id(1) == 0)
    def _():
        for w_ref, wb in zip(w_refs, wb_refs):
            wb[...] = w_ref[...].astype(_BF16)

    a = a_ref[...]
    accs = [jnp.dot(a, wb[...], preferred_element_type=_F32) for wb in wb_refs]
    if epilogue == "glu":
        r = accs[0] * _sigmoid(accs[1])
    else:
        r = _epilogue(epilogue, accs[0])
    o_ref[...] = r.astype(o_ref.dtype)


def _proj(a, w, *, offs, n_cols, epilogue, out_dtype, tn, tm=1024, name):
    m, kdim = a.shape
    tm = min(tm, m)
    n_w = len(offs)
    w_specs = [pl.BlockSpec((kdim, tn), functools.partial(lambda j, i, jo: (0, j + jo), jo=off // tn))
               for off in offs]
    est = 2 * _nbytes((tm, kdim), a.dtype) + n_w * (2 * _nbytes((kdim, tn), _F32) + _nbytes((kdim, tn), _BF16))
    est += 2 * _nbytes((tm, tn), out_dtype) + (n_w + 1) * _nbytes((tm, tn), _F32)
    return pl.pallas_call(
        functools.partial(_proj_kernel, epilogue=epilogue, n_w=n_w),
        grid=(n_cols // tn, m // tm),
        in_specs=[pl.BlockSpec((tm, kdim), lambda j, i: (i, 0))] + w_specs,
        out_specs=pl.BlockSpec((tm, tn), lambda j, i: (i, j)),
        out_shape=jax.ShapeDtypeStruct((m, n_cols), out_dtype),
        scratch_shapes=[pltpu.VMEM((kdim, tn), _BF16)] * n_w,
        compiler_params=pltpu.CompilerParams(dimension_semantics=("parallel", "arbitrary"),
                                             vmem_limit_bytes=_vmem_limit(est)),
        name=name,
    )(a, *([w] * n_w))


def _merge_kernel(m_ref, cf_ref, wl_ref, wc_ref, gl_ref, gc_ref, o_ref, acc_ref, *, nk):
    k = pl.program_id(2)

    @pl.when(k == 0)
    def _():
        acc_ref[...] = jnp.zeros_like(acc_ref)

    lo = jnp.dot(m_ref[...], wl_ref[...], preferred_element_type=_F32)
    co = jnp.dot(cf_ref[...], wc_ref[...], preferred_element_type=_F32)
    acc_ref[...] += gl_ref[...].astype(_F32) * lo + gc_ref[...].astype(_F32) * co

    @pl.when(k == nk - 1)
    def _():
        o_ref[...] = acc_ref[...].astype(o_ref.dtype)


def _merge_mm(m_act, cf_act, w_lru, w_conf, gates, *, tm=1024, tn=1024, tk=2048):
    m, kdim = m_act.shape
    n = w_lru.shape[1]
    nk = kdim // tk
    jc = n // tn
    est = 4 * _nbytes((tm, tk), _BF16) + 4 * _nbytes((tk, tn), _BF16) + 4 * _nbytes((tm, tn), _BF16)
    est += 2 * _nbytes((tm, tn), _BF16) + 4 * _nbytes((tm, tn), _F32)
    return pl.pallas_call(
        functools.partial(_merge_kernel, nk=nk),
        grid=(m // tm, n // tn, nk),
        in_specs=[pl.BlockSpec((tm, tk), lambda i, j, k: (i, k)),
                  pl.BlockSpec((tm, tk), lambda i, j, k: (i, k)),
                  pl.BlockSpec((tk, tn), lambda i, j, k: (k, j)),
                  pl.BlockSpec((tk, tn), lambda i, j, k: (k, j)),
                  pl.BlockSpec((tm, tn), lambda i, j, k: (i, j)),
                  pl.BlockSpec((tm, tn), lambda i, j, k: (i, j + jc))],
        out_specs=pl.BlockSpec((tm, tn), lambda i, j, k: (i, j)),
        out_shape=jax.ShapeDtypeStruct((m, n), _BF16),
        scratch_shapes=[pltpu.VMEM((tm, tn), _F32)],
        compiler_params=pltpu.CompilerParams(
            dimension_semantics=("parallel", "parallel", "arbitrary"),
            vmem_limit_bytes=_vmem_limit(est)),
        name="gated_merge",
    )(m_act, cf_act, w_lru, w_conf, gates, gates)


def _lru_kernel(x_ref, cw_ref, cb_ref, wa_ref, ba_ref, wi_ref, bi_ref, lam_ref, h0_ref, *rest,
                reverse, combine, tt, heads_per_block):
    if combine:
        hf_ref, gate_ref, o_ref, px, xcp, xedge, carry = rest
    else:
        o_ref, px, xcp, xedge, carry = rest
    cbw = x_ref.shape[1]
    slabs = cbw // _V7X_LANES
    sub = _V7X_SUBLANES
    ln = tt // sub
    pitch = ln + sub
    nhalo = _LRU_CONV - 1
    base = 0 if reverse else nhalo * sub
    i = pl.program_id(2)
    srow = lax.broadcasted_iota(jnp.int32, (sub, _V7X_LANES), 0)

    @pl.when(i == 0)
    def _():
        carry[...] = h0_ref[0]
        xedge[...] = jnp.zeros_like(xedge)

    for s in range(slabs):
        lanes = slice(s * _V7X_LANES, (s + 1) * _V7X_LANES)
        for c in range(sub):
            px[s, c * pitch:c * pitch + ln, :] = x_ref[c * ln:(c + 1) * ln, lanes]
        for j in range(ln):
            xcp[s, base + j * sub:base + (j + 1) * sub, :] = px[s, pl.ds(j, sub, stride=pitch), :]
        for m in range(nhalo):
            prev = xedge[s, m * sub:(m + 1) * sub, :]
            if reverse:
                cur = xcp[s, m * sub:(m + 1) * sub, :]
                xedge[s, m * sub:(m + 1) * sub, :] = cur
                xcp[s, (ln + m) * sub:(ln + m + 1) * sub, :] = pltpu.roll(jnp.where(srow == 0, prev, cur), sub - 1, 0)
            else:
                cur = xcp[s, base + (ln - nhalo + m) * sub:base + (ln - nhalo + m + 1) * sub, :]
                xedge[s, m * sub:(m + 1) * sub, :] = cur
                xcp[s, m * sub:(m + 1) * sub, :] = pltpu.roll(jnp.where(srow == sub - 1, prev, cur), 1, 0)

    def conv_slab(s):
        lanes = slice(s * _V7X_LANES, (s + 1) * _V7X_LANES)
        y = jnp.broadcast_to(cb_ref[:, lanes], (tt, _V7X_LANES))
        for k in range(_LRU_CONV):
            off = (nhalo - k) if reverse else k
            y = y + cw_ref[k:k + 1, lanes] * xcp[s, off * sub:off * sub + tt, :]
        return y

    lam = lam_ref[...]
    z = -lam
    softplus = jnp.maximum(z, 0.0) + jnp.log1p(jnp.exp(-jnp.abs(z)))
    c8 = (-_LRU_C * _LOG2E) * softplus
    hw = cbw // heads_per_block
    order = range(ln - 1, -1, -1) if reverse else range(ln)
    chunk_order = range(sub - 1, -1, -1) if reverse else range(sub)
    for hh in range(heads_per_block):
        cs = slice(hh * hw, (hh + 1) * hw)
        head_slabs = range(hh * hw // _V7X_LANES, (hh + 1) * hw // _V7X_LANES)
        yh = jnp.concatenate([conv_slab(s) for s in head_slabs], axis=1)
        yb = yh.astype(_BF16)
        tr = jnp.tanh(jnp.dot(yb, wa_ref[hh] * 0.5, preferred_element_type=_F32) + 0.5 * ba_ref[:, cs])
        tg = jnp.tanh(jnp.dot(yb, wi_ref[hh] * 0.5, preferred_element_type=_F32) + 0.5 * bi_ref[:, cs])
        c8h = 0.5 * c8[:, cs]
        a = jnp.exp2(c8h * tr + c8h)
        v = 1.0 - a * a
        b = jnp.where(v > 0.0, v * lax.rsqrt(v), 0.0) * (0.5 * tg + 0.5) * yh

        h = jnp.zeros((sub, hw), _F32)
        p = jnp.ones((sub, hw), _F32)
        hs, ps = [None] * ln, [None] * ln
        for j in order:
            aj = a[j * sub:(j + 1) * sub]
            h = aj * h + b[j * sub:(j + 1) * sub]
            p = aj * p
            hs[j], ps[j] = h, p
        init = carry[:, cs]
        inits = [None] * sub
        for c in chunk_order:
            inits[c] = init
            init = p[c:c + 1, :] * init + h[c:c + 1, :]
        carry[:, cs] = init
        init_all = jnp.concatenate(inits, axis=0)
        for j in range(ln):
            hj = ps[j] * init_all + hs[j]
            for n, s in enumerate(head_slabs):
                px[s, pl.ds(j, sub, stride=pitch), :] = hj[:, n * _V7X_LANES:(n + 1) * _V7X_LANES]
        for s in head_slabs:
            lanes = slice(s * _V7X_LANES, (s + 1) * _V7X_LANES)
            for c in range(sub):
                rows = slice(c * ln, (c + 1) * ln)
                hn = px[s, c * pitch:c * pitch + ln, :]
                if combine:
                    hn = (hn + hf_ref[rows, lanes]) * gate_ref[rows, lanes].astype(_F32)
                o_ref[rows, lanes] = hn.astype(o_ref.dtype)


def _lru(x2d, cw, cb, wa, ba, wi, bi, lam, h0, *, batch, reverse, hf=None, gate=None,
         tt=512, heads_per_block=2):
    rows, r = x2d.shape
    t = rows // batch
    tt = min(tt, t)
    nt = t // tt
    hw = r // _LRU_HEADS
    cbw = hw * heads_per_block
    slabs = cbw // _V7X_LANES
    ln = tt // _V7X_SUBLANES
    combine = hf is not None

    def row_map(b, h, i):
        ti = (nt - 1 - i) if reverse else i
        return (b * nt + ti, h)

    chan = lambda b, h, i: (0, h)
    in_specs = [pl.BlockSpec((tt, cbw), row_map),
                pl.BlockSpec((_LRU_CONV, cbw), chan),
                pl.BlockSpec((1, cbw), chan),
                pl.BlockSpec((heads_per_block, hw, hw), lambda b, h, i: (h, 0, 0)),
                pl.BlockSpec((1, cbw), chan),
                pl.BlockSpec((heads_per_block, hw, hw), lambda b, h, i: (h, 0, 0)),
                pl.BlockSpec((1, cbw), chan),
                pl.BlockSpec((1, cbw), chan),
                pl.BlockSpec((1, 1, cbw), lambda b, h, i: (b, 0, h))]
    args = [x2d, cw, cb, wa, ba, wi, bi, lam, h0]
    if combine:
        in_specs += [pl.BlockSpec((tt, cbw), row_map), pl.BlockSpec((tt, cbw), row_map)]
        args += [hf, gate]
    px_shape = (slabs, _V7X_SUBLANES * (ln + _V7X_SUBLANES), _V7X_LANES)
    xcp_shape = (slabs, (ln + _LRU_CONV - 1) * _V7X_SUBLANES, _V7X_LANES)
    edge_shape = (slabs, (_LRU_CONV - 1) * _V7X_SUBLANES, _V7X_LANES)
    scratch = [pltpu.VMEM(px_shape, _F32), pltpu.VMEM(xcp_shape, _F32), pltpu.VMEM(edge_shape, _F32),
               pltpu.VMEM((1, cbw), _F32)]
    tile = _nbytes((tt, cbw), _F32)
    est = 8 * tile + _nbytes(px_shape, _F32) + _nbytes(xcp_shape, _F32) + 8 * tile
    return pl.pallas_call(
        functools.partial(_lru_kernel, reverse=reverse, combine=combine, tt=tt,
                          heads_per_block=heads_per_block),
        grid=(batch, _LRU_HEADS // heads_per_block, nt),
        in_specs=in_specs,
        out_specs=pl.BlockSpec((tt, cbw), row_map),
        out_shape=jax.ShapeDtypeStruct((rows, r), _BF16 if combine else _F32),
        scratch_shapes=scratch,
        compiler_params=pltpu.CompilerParams(
            dimension_semantics=("parallel", "parallel", "arbitrary"),
            vmem_limit_bytes=_vmem_limit(est)),
        name="rglru_bwd" if reverse else "rglru_fwd",
    )(*args)


def _conf_conv_kernel(v_ref, w_ref, b_ref, o_ref, z, pt, *, n, cb, n_col_blocks, lines_per_step):
    slabs = cb // _V7X_LANES
    pitch = n + _V7X_SUBLANES
    c = pl.program_id(1)
    zeros = jnp.zeros((_CONF_PAD * n, _V7X_LANES), _F32)

    def conv(s, write_line_block):
        def step(jj, carry):
            l0 = jj * lines_per_step
            accs = [jnp.broadcast_to(b_ref[:, s * _V7X_LANES:(s + 1) * _V7X_LANES], (n, _V7X_LANES))] * lines_per_step
            for m in range(lines_per_step + _CONF_K - 1):
                line = z[s, pl.ds(pl.multiple_of((l0 + m) * n, n), n), :]
                for l in range(lines_per_step):
                    k = m - l
                    if 0 <= k < _CONF_K:
                        accs[l] = accs[l] + w_ref[k:k + 1, s * _V7X_LANES:(s + 1) * _V7X_LANES] * line
            write_line_block(l0, accs)
            return carry
        lax.fori_loop(0, n // lines_per_step, step, 0)

    for s in range(slabs):
        lanes = slice(s * _V7X_LANES, (s + 1) * _V7X_LANES)
        z[s, 0:_CONF_PAD * n, :] = zeros
        z[s, (_CONF_PAD + n) * n:, :] = zeros

        @pl.when(c < n_col_blocks)
        def _():
            for i in range(n):
                pt[s, i * pitch:i * pitch + n, :] = v_ref[0, i * n:(i + 1) * n, lanes]
            for j in range(n):
                for i0 in range(n // _V7X_SUBLANES):
                    r0 = (_CONF_PAD + j) * n + i0 * _V7X_SUBLANES
                    z[s, r0:r0 + _V7X_SUBLANES, :] = pt[
                        s, pl.ds(i0 * _V7X_SUBLANES * pitch + j, _V7X_SUBLANES, stride=pitch), :]

            def write(l0, accs):
                for l in range(lines_per_step):
                    pt[s, pl.ds(pl.multiple_of((l0 + l) * pitch, _V7X_SUBLANES), n), :] = accs[l]
            conv(s, write)

            rows_per_store = 2 * _V7X_SUBLANES
            for i in range(n):
                for j0 in range(0, n, rows_per_store):
                    halves = [pt[s, pl.ds((j0 + h) * pitch + i, _V7X_SUBLANES, stride=pitch), :]
                              for h in range(0, rows_per_store, _V7X_SUBLANES)]
                    r0 = i * n + j0
                    o_ref[0, r0:r0 + rows_per_store, lanes] = jnp.concatenate(halves, axis=0).astype(o_ref.dtype)

        @pl.when(c >= n_col_blocks)
        def _():
            z[s, _CONF_PAD * n:(_CONF_PAD + n) * n, :] = v_ref[0, :, lanes]

            def write(l0, accs):
                for l in range(lines_per_step):
                    o_ref[0, pl.ds(pl.multiple_of((l0 + l) * n, n), n), lanes] = accs[l].astype(o_ref.dtype)
            conv(s, write)


def _conf_conv(v3d, w, b, *, n, cb=256, lines_per_step=4):
    bsz, t, ch = v3d.shape
    assert t == n * n and (ch // 2) % cb == 0
    n_col_blocks = (ch // 2) // cb
    slabs = cb // _V7X_LANES
    z_shape = (slabs, (n + 2 * _CONF_PAD) * n, _V7X_LANES)
    pt_shape = (slabs, n * (n + _V7X_SUBLANES), _V7X_LANES)
    est = 4 * _nbytes((t, cb), _F32) + _nbytes(z_shape, _F32) + _nbytes(pt_shape, _F32)
    return pl.pallas_call(
        functools.partial(_conf_conv_kernel, n=n, cb=cb, n_col_blocks=n_col_blocks,
                          lines_per_step=lines_per_step),
        grid=(bsz, ch // cb),
        in_specs=[pl.BlockSpec((1, t, cb), lambda i, c: (i, 0, c)),
                  pl.BlockSpec((_CONF_K, cb), lambda i, c: (0, c)),
                  pl.BlockSpec((1, cb), lambda i, c: (0, c))],
        out_specs=pl.BlockSpec((1, t, cb), lambda i, c: (i, 0, c)),
        out_shape=jax.ShapeDtypeStruct((bsz, t, ch), _BF16),
        scratch_shapes=[pltpu.VMEM(z_shape, _F32), pltpu.VMEM(pt_shape, _F32)],
        compiler_params=pltpu.CompilerParams(dimension_semantics=("parallel", "parallel"),
                                             vmem_limit_bytes=_vmem_limit(est)),
        name="conformer_dwconv",
    )(v3d, w, b)


def _ln_silu_kernel(x_ref, g_ref, b_ref, o_ref):
    x = x_ref[...].astype(_F32)
    mu = jnp.mean(x, axis=-1, keepdims=True)
    xc = x - mu
    y = xc * lax.rsqrt(jnp.mean(xc * xc, axis=-1, keepdims=True) + _EPS)
    y = y * g_ref[...] + b_ref[...]
    o_ref[...] = (y * _sigmoid(y)).astype(o_ref.dtype)


def _ln_silu(x, g, b, tt=512):
    m, d = x.shape
    est = 2 * _nbytes((tt, d), x.dtype) + 2 * _nbytes((tt, d), _BF16) + 4 * _nbytes((tt, d), _F32)
    return pl.pallas_call(
        _ln_silu_kernel,
        grid=(m // tt,),
        in_specs=[pl.BlockSpec((tt, d), lambda i: (i, 0)),
                  pl.BlockSpec((1, d), lambda i: (0, 0)),
                  pl.BlockSpec((1, d), lambda i: (0, 0))],
        out_specs=pl.BlockSpec((tt, d), lambda i: (i, 0)),
        out_shape=jax.ShapeDtypeStruct((m, d), _BF16),
        compiler_params=pltpu.CompilerParams(dimension_semantics=("parallel",),
                                             vmem_limit_bytes=_vmem_limit(est)),
        name="layernorm_swish",
    )(x, g, b)


def kernel(x, c, ctx, c_ctx, w_ada, b_ada, norm1_g, w_in, lru_conv_w, lru_conv_b, lru_w_a, lru_b_a,
           lru_w_i, lru_b_i, lru_lam, w_lru_out, conf_dw_w, conf_dw_b, conf_ln_g, conf_ln_b,
           w_conf_out, w_o, norm2_g, w_ff1, w_ff2, final_g):
    assert w_ada.shape[0] == 1, "single-layer stack only (the context stream is then never updated)"
    bsz, t, d = x.shape
    tc = ctx.shape[1]
    r = lru_lam.shape[-1]
    cw = conf_dw_w.shape[-1]
    row = lambda v: v.reshape(1, -1)

    cc = jnp.concatenate([c, c_ctx[None, :], jnp.zeros((_V7X_SUBLANES - bsz - 1, d), _F32)], axis=0)
    mod = _ada(cc, w_ada[0], row(b_ada[0]))
    mod_l = mod[:bsz].reshape(bsz, 6, 1, d)
    sh1, sc1, g1, sh2, sc2, g2 = (mod_l[:, k] for k in range(6))
    mod_c = jnp.broadcast_to(mod[bsz].reshape(6, 1, 1, d), (6, bsz, 1, d))
    csh1, csc1 = mod_c[0], mod_c[1]

    u = _norm_mod(x, row(norm1_g[0]), sc1, sh1).reshape(bsz * t, d)
    uc = _norm_mod(ctx, row(norm1_g[0]), csc1, csh1).reshape(bsz * tc, d)

    w_in0 = w_in[0]
    gate_act = _proj(u, w_in0, offs=(0,), n_cols=r, epilogue="gelu", out_dtype=_BF16, tn=512,
                     name="in_proj_gate")
    x_br = _proj(u, w_in0, offs=(r,), n_cols=r, epilogue="none", out_dtype=_F32, tn=512, name="in_proj_lru")
    v = _proj(u, w_in0, offs=(2 * r, 2 * r + cw), n_cols=cw, epilogue="glu", out_dtype=_F32, tn=256,
              name="in_proj_glu")
    br_gates = _proj(u, w_in0, offs=(2 * r + 2 * cw,), n_cols=2 * d, epilogue="sigmoid", out_dtype=_BF16,
                     tn=512, name="in_proj_branch")
    xc_br = _proj(uc, w_in0, offs=(r,), n_cols=r, epilogue="none", out_dtype=_F32, tn=512, name="in_proj_ctx")

    def lru_params(dirn):
        return (lru_conv_w[0, dirn], row(lru_conv_b[0, dirn]), lru_w_a[0, dirn].astype(_BF16),
                row(lru_b_a[0, dirn]), lru_w_i[0, dirn].astype(_BF16), row(lru_b_i[0, dirn]),
                row(lru_lam[0, dirn]))

    zero_state = jnp.zeros((bsz, 1, r), _F32)
    hc_f = _lru(xc_br, *lru_params(0), zero_state, batch=bsz, reverse=False)
    hc_b = _lru(xc_br, *lru_params(1), zero_state, batch=bsz, reverse=True)
    h0_f = hc_f.reshape(bsz, tc, r)[:, tc - 1:tc, :]
    h0_b = hc_b.reshape(bsz, tc, r)[:, 0:1, :]
    h_f = _lru(x_br, *lru_params(0), h0_f, batch=bsz, reverse=False)
    lru_act = _lru(x_br, *lru_params(1), h0_b, batch=bsz, reverse=True, hf=h_f, gate=gate_act)

    conv = _conf_conv(v.reshape(bsz, t, cw), conf_dw_w[0], row(conf_dw_b[0]), n=_GRID_W)
    conf_act = _ln_silu(conv.reshape(bsz * t, cw), row(conf_ln_g[0]), row(conf_ln_b[0]))

    merged = _merge_mm(lru_act, conf_act, w_lru_out[0].astype(_BF16), w_conf_out[0].astype(_BF16), br_gates)
    x1 = _mm(merged, w_o[0].astype(_BF16), epilogue="resid", resid=x.reshape(bsz * t, d), gate=g1,
             rows_per_batch=t, tn=512, name="out_proj_residual")

    u2 = _norm_mod(x1.reshape(bsz, t, d), row(norm2_g[0]), sc2, sh2).reshape(bsz * t, d)
    hid = _mm(u2, w_ff1[0].astype(_BF16), epilogue="relu2", out_dtype=_BF16, name="mlp_up")
    x2 = _mm(hid, w_ff2[0].astype(_BF16), epilogue="resid", resid=x1, gate=g2, rows_per_batch=t,
             tk=4096, name="mlp_down_residual")
    return _final_norm(x2, row(final_g)).reshape(bsz, t, d)
```

```python
import functools

import jax
import jax.numpy as jnp
from jax import lax
from jax.experimental import pallas as pl
from jax.experimental.pallas import tpu as pltpu

_F32 = jnp.float32
_BF16 = jnp.bfloat16

_GRID_W = 64
_LRU_HEADS = 16
_LRU_CONV = 4
_LRU_C = 8.0
_CONF_K = 31
_CONF_PAD = (_CONF_K - 1) // 2
_EPS = 1e-6
_LOG2E = 1.4426950408889634

_V7X_SUBLANES = 8
_V7X_LANES = 128
_V7X_SCOPED_VMEM_BYTES = 60000 * 1024


_COMPILER_SCRATCH_BYTES = 2 << 20


def _vmem_limit(nbytes):
    return int(min(nbytes + _COMPILER_SCRATCH_BYTES, _V7X_SCOPED_VMEM_BYTES))


def _nbytes(shape, dtype):
    n = 1
    for s in shape:
        n *= s
    return n * jnp.dtype(dtype).itemsize


def _sigmoid(x):
    return 0.5 * jnp.tanh(0.5 * x) + 0.5


def _ada_kernel(cc_ref, w_ref, b_ref, o_ref):
    cc = cc_ref[...]
    s = cc * _sigmoid(cc)
    o_ref[...] = jnp.dot(s.astype(_BF16), w_ref[...].astype(_BF16),
                         preferred_element_type=_F32) + b_ref[...]


def _ada(cc, w, b, tn=512):
    m, d = cc.shape
    n = w.shape[1]
    est = 2 * _nbytes((d, tn), _F32) + _nbytes((d, tn), _BF16) + 4 * _nbytes((m, d), _F32) + (4 << 20)
    return pl.pallas_call(
        _ada_kernel,
        grid=(n // tn,),
        in_specs=[pl.BlockSpec((m, d), lambda j: (0, 0)),
                  pl.BlockSpec((d, tn), lambda j: (0, j)),
                  pl.BlockSpec((1, tn), lambda j: (0, j))],
        out_specs=pl.BlockSpec((m, tn), lambda j: (0, j)),
        out_shape=jax.ShapeDtypeStruct((m, n), _F32),
        compiler_params=pltpu.CompilerParams(dimension_semantics=("parallel",),
                                             vmem_limit_bytes=_vmem_limit(est)),
        name="ada_ln",
    )(cc, w, b)


def _norm_mod_kernel(x_ref, g_ref, sc_ref, sh_ref, o_ref):
    x = x_ref[0]
    ms = jnp.mean(x * x, axis=-1, keepdims=True)
    y = x * lax.rsqrt(ms + _EPS) * g_ref[...]
    o_ref[0] = (y * (1.0 + sc_ref[0]) + sh_ref[0]).astype(o_ref.dtype)


def _norm_mod(x, g, sc, sh, tt=512):
    b, t, d = x.shape
    tt = min(tt, t)
    est = 2 * _nbytes((tt, d), _F32) + 2 * _nbytes((tt, d), _BF16) + 2 * _nbytes((tt, d), _F32)
    return pl.pallas_call(
        _norm_mod_kernel,
        grid=(b, t // tt),
        in_specs=[pl.BlockSpec((1, tt, d), lambda i, j: (i, j, 0)),
                  pl.BlockSpec((1, d), lambda i, j: (0, 0)),
                  pl.BlockSpec((1, 1, d), lambda i, j: (i, 0, 0)),
                  pl.BlockSpec((1, 1, d), lambda i, j: (i, 0, 0))],
        out_specs=pl.BlockSpec((1, tt, d), lambda i, j: (i, j, 0)),
        out_shape=jax.ShapeDtypeStruct((b, t, d), _BF16),
        compiler_params=pltpu.CompilerParams(dimension_semantics=("parallel", "parallel"),
                                             vmem_limit_bytes=_vmem_limit(est)),
        name="rmsnorm_modulate",
    )(x, g, sc, sh)


def _final_norm_kernel(x_ref, g_ref, o_ref):
    x = x_ref[...]
    ms = jnp.mean(x * x, axis=-1, keepdims=True)
    o_ref[...] = x * lax.rsqrt(ms + _EPS) * g_ref[...]


def _final_norm(x, g, tt=512):
    m, d = x.shape
    est = 4 * _nbytes((tt, d), _F32) + 2 * _nbytes((tt, d), _F32)
    return pl.pallas_call(
        _final_norm_kernel,
        grid=(m // tt,),
        in_specs=[pl.BlockSpec((tt, d), lambda i: (i, 0)),
                  pl.BlockSpec((1, d), lambda i: (0, 0))],
        out_specs=pl.BlockSpec((tt, d), lambda i: (i, 0)),
        out_shape=jax.ShapeDtypeStruct((m, d), _F32),
        compiler_params=pltpu.CompilerParams(dimension_semantics=("parallel",),
                                             vmem_limit_bytes=_vmem_limit(est)),
        name="final_rmsnorm",
    )(x, g)


def _side_cast_body(body, n_in, *refs):
    side_in, o_ref, side_out = refs[n_in], refs[n_in + 1], refs[n_in + 2]
    side_out[...] = side_in[...].astype(_BF16)
    body(*refs[:n_in], o_ref, *refs[n_in + 3:])


def _matmul_call(body, *, grid, in_specs, out_spec, out_shape, scratch, semantics, est, name, args, side=None):
    if side is None:
        return pl.pallas_call(
            body, grid=grid, in_specs=in_specs, out_specs=out_spec, out_shape=out_shape, scratch_shapes=scratch,
            compiler_params=pltpu.CompilerParams(dimension_semantics=semantics, vmem_limit_bytes=_vmem_limit(est)),
            name=name)(*args)
    n_steps = 1
    for g in grid:
        n_steps *= g
    rows, cols = side.shape
    rb = rows // n_steps
    assert rb * n_steps == rows and rb % (2 * _V7X_SUBLANES) == 0, (side.shape, grid)

    def step_block(*idx):
        lin = 0
        for g, ix in zip(grid, idx):
            lin = lin * g + ix
        return (lin, 0)

    side_spec = pl.BlockSpec((rb, cols), step_block)
    est += 2 * _nbytes((rb, cols), _F32) + 2 * _nbytes((rb, cols), _BF16)
    return pl.pallas_call(
        functools.partial(_side_cast_body, body, len(in_specs)),
        grid=grid, in_specs=in_specs + [side_spec], out_specs=[out_spec, side_spec],
        out_shape=[out_shape, jax.ShapeDtypeStruct(side.shape, _BF16)], scratch_shapes=scratch,
        compiler_params=pltpu.CompilerParams(dimension_semantics=semantics, vmem_limit_bytes=_vmem_limit(est)),
        name=name)(*args, side)


def _epilogue(name, acc, x_ref=None, g_ref=None):
    if name == "none":
        return acc
    if name == "gelu":
        return jax.nn.gelu(acc)
    if name == "sigmoid":
        return _sigmoid(acc)
    if name == "relu2":
        return jnp.square(jnp.maximum(acc, 0.0))
    if name == "resid":
        return x_ref[...] + g_ref[0] * acc
    raise ValueError(name)


def _mm_kernel(a_ref, w_ref, *rest, epilogue, nk):
    if epilogue == "resid":
        x_ref, g_ref, o_ref = rest[:3]
        scratch = rest[3:]
    else:
        x_ref = g_ref = None
        o_ref = rest[0]
        scratch = rest[1:]
    if nk == 1:
        acc = jnp.dot(a_ref[...], w_ref[...], preferred_element_type=_F32)
        o_ref[...] = _epilogue(epilogue, acc, x_ref, g_ref).astype(o_ref.dtype)
        return
    acc_ref = scratch[0] if scratch else o_ref
    k = pl.program_id(2)

    @pl.when(k == 0)
    def _():
        acc_ref[...] = jnp.zeros_like(acc_ref)

    acc_ref[...] += jnp.dot(a_ref[...], w_ref[...], preferred_element_type=_F32)

    @pl.when(k == nk - 1)
    def _():
        o_ref[...] = _epilogue(epilogue, acc_ref[...], x_ref, g_ref).astype(o_ref.dtype)


def _mm(a, w, *, n_off=0, n_cols=None, epilogue="none", out_dtype=_F32, resid=None, gate=None,
        rows_per_batch=None, tm=1024, tn=1024, tk=None, side=None, name="matmul"):
    m, kdim = a.shape
    n_cols = w.shape[1] if n_cols is None else n_cols
    tk = kdim if tk is None else tk
    tm = min(tm, m)
    nk = kdim // tk
    joff = n_off // tn
    grid = (m // tm, n_cols // tn, nk)
    in_specs = [pl.BlockSpec((tm, tk), lambda i, j, k: (i, k)),
                pl.BlockSpec((tk, tn), lambda i, j, k: (k, j + joff))]
    args = [a, w]
    est = 2 * _nbytes((tm, tk), a.dtype) + 2 * _nbytes((tk, tn), w.dtype)
    est += 2 * _nbytes((tm, tn), out_dtype) + 3 * _nbytes((tm, tn), _F32)
    if epilogue == "resid":
        tiles_per_batch = rows_per_batch // tm
        in_specs += [pl.BlockSpec((tm, tn), lambda i, j, k: (i, j)),
                     pl.BlockSpec((1, 1, tn), lambda i, j, k: (i // tiles_per_batch, 0, j))]
        args += [resid, gate]
        est += 2 * _nbytes((tm, tn), _F32)
    scratch = []
    if nk > 1 and out_dtype != _F32:
        scratch = [pltpu.VMEM((tm, tn), _F32)]
        est += _nbytes((tm, tn), _F32)
    return _matmul_call(
        functools.partial(_mm_kernel, epilogue=epilogue, nk=nk), grid=grid, in_specs=in_specs,
        out_spec=pl.BlockSpec((tm, tn), lambda i, j, k: (i, j)),
        out_shape=jax.ShapeDtypeStruct((m, n_cols), out_dtype), scratch=scratch,
        semantics=("parallel", "parallel", "arbitrary"), est=est, name=name, args=args, side=side)


def _proj_kernel(a_ref, *rest, epilogue, n_w):
    w_refs, o_ref, wb_refs = rest[:n_w], rest[n_w], rest[n_w + 1:]

    @pl.when(pl.program_id(1) == 0)
    def _():
        for w_ref, wb in zip(w_refs, wb_refs):
            wb[...] = w_ref[...].astype(_BF16)

    a = a_ref[...]
    accs = [jnp.dot(a, wb[...], preferred_element_type=_F32) for wb in wb_refs]
    if epilogue == "glu":
        r = accs[0] * _sigmoid(accs[1])
    else:
        r = _epilogue(epilogue, accs[0])
    o_ref[...] = r.astype(o_ref.dtype)


def _proj(a, w, *, offs, n_cols, epilogue, out_dtype, tn, tm=1024, side=None, name):
    m, kdim = a.shape
    tm = min(tm, m)
    n_w = len(offs)
    w_specs = [pl.BlockSpec((kdim, tn), functools.partial(lambda j, i, jo: (0, j + jo), jo=off // tn))
               for off in offs]
    est = 2 * _nbytes((tm, kdim), a.dtype) + n_w * (2 * _nbytes((kdim, tn), _F32) + _nbytes((kdim, tn), _BF16))
    est += 2 * _nbytes((tm, tn), out_dtype) + (n_w + 1) * _nbytes((tm, tn), _F32)
    return _matmul_call(
        functools.partial(_proj_kernel, epilogue=epilogue, n_w=n_w), grid=(n_cols // tn, m // tm),
        in_specs=[pl.BlockSpec((tm, kdim), lambda j, i: (i, 0))] + w_specs,
        out_spec=pl.BlockSpec((tm, tn), lambda j, i: (i, j)),
        out_shape=jax.ShapeDtypeStruct((m, n_cols), out_dtype),
        scratch=[pltpu.VMEM((kdim, tn), _BF16)] * n_w, semantics=("parallel", "arbitrary"), est=est,
        name=name, args=[a] + [w] * n_w, side=side)


def _merge_kernel(m_ref, cf_ref, wl_ref, wc_ref, gl_ref, gc_ref, o_ref, acc_ref, *, nk):
    k = pl.program_id(2)

    @pl.when(k == 0)
    def _():
        acc_ref[...] = jnp.zeros_like(acc_ref)

    lo = jnp.dot(m_ref[...], wl_ref[...], preferred_element_type=_F32)
    co = jnp.dot(cf_ref[...], wc_ref[...], preferred_element_type=_F32)
    acc_ref[...] += gl_ref[...].astype(_F32) * lo + gc_ref[...].astype(_F32) * co

    @pl.when(k == nk - 1)
    def _():
        o_ref[...] = acc_ref[...].astype(o_ref.dtype)


def _merge_mm(m_act, cf_act, w_lru, w_conf, gates, *, tm=1024, tn=1024, tk=2048):
    m, kdim = m_act.shape
    n = w_lru.shape[1]
    nk = kdim // tk
    jc = n // tn
    est = 4 * _nbytes((tm, tk), _BF16) + 4 * _nbytes((tk, tn), _BF16) + 4 * _nbytes((tm, tn), _BF16)
    est += 2 * _nbytes((tm, tn), _BF16) + 4 * _nbytes((tm, tn), _F32)
    return pl.pallas_call(
        functools.partial(_merge_kernel, nk=nk),
        grid=(m // tm, n // tn, nk),
        in_specs=[pl.BlockSpec((tm, tk), lambda i, j, k: (i, k)),
                  pl.BlockSpec((tm, tk), lambda i, j, k: (i, k)),
                  pl.BlockSpec((tk, tn), lambda i, j, k: (k, j)),
                  pl.BlockSpec((tk, tn), lambda i, j, k: (k, j)),
                  pl.BlockSpec((tm, tn), lambda i, j, k: (i, j)),
                  pl.BlockSpec((tm, tn), lambda i, j, k: (i, j + jc))],
        out_specs=pl.BlockSpec((tm, tn), lambda i, j, k: (i, j)),
        out_shape=jax.ShapeDtypeStruct((m, n), _BF16),
        scratch_shapes=[pltpu.VMEM((tm, tn), _F32)],
        compiler_params=pltpu.CompilerParams(
            dimension_semantics=("parallel", "parallel", "arbitrary"),
            vmem_limit_bytes=_vmem_limit(est)),
        name="gated_merge",
    )(m_act, cf_act, w_lru, w_conf, gates, gates)


def _lru_kernel(x_ref, cw_ref, cb_ref, wa_ref, ba_ref, wi_ref, bi_ref, lam_ref, h0_ref, *rest,
                reverse, combine, tt, heads_per_block):
    if combine:
        hf_ref, gate_ref, o_ref, px, xcp, xedge, carry = rest
    else:
        o_ref, px, xcp, xedge, carry = rest
    cbw = x_ref.shape[1]
    slabs = cbw // _V7X_LANES
    sub = _V7X_SUBLANES
    ln = tt // sub
    pitch = ln + sub
    nhalo = _LRU_CONV - 1
    base = 0 if reverse else nhalo * sub
    i = pl.program_id(2)
    srow = lax.broadcasted_iota(jnp.int32, (sub, _V7X_LANES), 0)

    @pl.when(i == 0)
    def _():
        carry[...] = h0_ref[0]
        xedge[...] = jnp.zeros_like(xedge)

    for s in range(slabs):
        lanes = slice(s * _V7X_LANES, (s + 1) * _V7X_LANES)
        for c in range(sub):
            px[s, c * pitch:c * pitch + ln, :] = x_ref[c * ln:(c + 1) * ln, lanes]
        for j in range(ln):
            xcp[s, base + j * sub:base + (j + 1) * sub, :] = px[s, pl.ds(j, sub, stride=pitch), :]
        for m in range(nhalo):
            prev = xedge[s, m * sub:(m + 1) * sub, :]
            if reverse:
                cur = xcp[s, m * sub:(m + 1) * sub, :]
                xedge[s, m * sub:(m + 1) * sub, :] = cur
                xcp[s, (ln + m) * sub:(ln + m + 1) * sub, :] = pltpu.roll(jnp.where(srow == 0, prev, cur), sub - 1, 0)
            else:
                cur = xcp[s, base + (ln - nhalo + m) * sub:base + (ln - nhalo + m + 1) * sub, :]
                xedge[s, m * sub:(m + 1) * sub, :] = cur
                xcp[s, m * sub:(m + 1) * sub, :] = pltpu.roll(jnp.where(srow == sub - 1, prev, cur), 1, 0)

    def conv_slab(s):
        lanes = slice(s * _V7X_LANES, (s + 1) * _V7X_LANES)
        y = jnp.broadcast_to(cb_ref[:, lanes], (tt, _V7X_LANES))
        for k in range(_LRU_CONV):
            off = (nhalo - k) if reverse else k
            y = y + cw_ref[k:k + 1, lanes] * xcp[s, off * sub:off * sub + tt, :]
        return y

    lam = lam_ref[...]
    z = -lam
    softplus = jnp.maximum(z, 0.0) + jnp.log1p(jnp.exp(-jnp.abs(z)))
    c8 = (-_LRU_C * _LOG2E) * softplus
    hw = cbw // heads_per_block
    order = range(ln - 1, -1, -1) if reverse else range(ln)
    chunk_order = range(sub - 1, -1, -1) if reverse else range(sub)
    for hh in range(heads_per_block):
        cs = slice(hh * hw, (hh + 1) * hw)
        head_slabs = range(hh * hw // _V7X_LANES, (hh + 1) * hw // _V7X_LANES)
        yh = jnp.concatenate([conv_slab(s) for s in head_slabs], axis=1)
        yb = yh.astype(_BF16)
        tr = jnp.tanh(jnp.dot(yb, wa_ref[hh] * 0.5, preferred_element_type=_F32) + 0.5 * ba_ref[:, cs])
        tg = jnp.tanh(jnp.dot(yb, wi_ref[hh] * 0.5, preferred_element_type=_F32) + 0.5 * bi_ref[:, cs])
        c8h = 0.5 * c8[:, cs]
        a = jnp.exp2(c8h * tr + c8h)
        v = 1.0 - a * a
        b = jnp.where(v > 0.0, v * lax.rsqrt(v), 0.0) * (0.5 * tg + 0.5) * yh

        h = jnp.zeros((sub, hw), _F32)
        p = jnp.ones((sub, hw), _F32)
        hs, ps = [None] * ln, [None] * ln
        for j in order:
            aj = a[j * sub:(j + 1) * sub]
            h = aj * h + b[j * sub:(j + 1) * sub]
            p = aj * p
            hs[j], ps[j] = h, p
        init = carry[:, cs]
        inits = [None] * sub
        for c in chunk_order:
            inits[c] = init
            init = p[c:c + 1, :] * init + h[c:c + 1, :]
        carry[:, cs] = init
        init_all = jnp.concatenate(inits, axis=0)
        for j in range(ln):
            hj = ps[j] * init_all + hs[j]
            for n, s in enumerate(head_slabs):
                px[s, pl.ds(j, sub, stride=pitch), :] = hj[:, n * _V7X_LANES:(n + 1) * _V7X_LANES]
        for s in head_slabs:
            lanes = slice(s * _V7X_LANES, (s + 1) * _V7X_LANES)
            for c in range(sub):
                rows = slice(c * ln, (c + 1) * ln)
                hn = px[s, c * pitch:c * pitch + ln, :]
                if combine:
                    hn = (hn + hf_ref[rows, lanes]) * gate_ref[rows, lanes].astype(_F32)
                o_ref[rows, lanes] = hn.astype(o_ref.dtype)


def _lru(x2d, cw, cb, wa, ba, wi, bi, lam, h0, *, batch, reverse, hf=None, gate=None,
         tt=512, heads_per_block=4):
    rows, r = x2d.shape
    t = rows // batch
    tt = min(tt, t)
    nt = t // tt
    hw = r // _LRU_HEADS
    cbw = hw * heads_per_block
    slabs = cbw // _V7X_LANES
    ln = tt // _V7X_SUBLANES
    combine = hf is not None

    def row_map(b, h, i):
        ti = (nt - 1 - i) if reverse else i
        return (b * nt + ti, h)

    chan = lambda b, h, i: (0, h)
    in_specs = [pl.BlockSpec((tt, cbw), row_map),
                pl.BlockSpec((_LRU_CONV, cbw), chan),
                pl.BlockSpec((1, cbw), chan),
                pl.BlockSpec((heads_per_block, hw, hw), lambda b, h, i: (h, 0, 0)),
                pl.BlockSpec((1, cbw), chan),
                pl.BlockSpec((heads_per_block, hw, hw), lambda b, h, i: (h, 0, 0)),
                pl.BlockSpec((1, cbw), chan),
                pl.BlockSpec((1, cbw), chan),
                pl.BlockSpec((1, 1, cbw), lambda b, h, i: (b, 0, h))]
    args = [x2d, cw, cb, wa, ba, wi, bi, lam, h0]
    if combine:
        in_specs += [pl.BlockSpec((tt, cbw), row_map), pl.BlockSpec((tt, cbw), row_map)]
        args += [hf, gate]
    px_shape = (slabs, _V7X_SUBLANES * (ln + _V7X_SUBLANES), _V7X_LANES)
    xcp_shape = (slabs, (ln + _LRU_CONV - 1) * _V7X_SUBLANES, _V7X_LANES)
    edge_shape = (slabs, (_LRU_CONV - 1) * _V7X_SUBLANES, _V7X_LANES)
    scratch = [pltpu.VMEM(px_shape, _F32), pltpu.VMEM(xcp_shape, _F32), pltpu.VMEM(edge_shape, _F32),
               pltpu.VMEM((1, cbw), _F32)]
    tile = _nbytes((tt, cbw), _F32)
    est = 8 * tile + _nbytes(px_shape, _F32) + _nbytes(xcp_shape, _F32) + 8 * tile
    return pl.pallas_call(
        functools.partial(_lru_kernel, reverse=reverse, combine=combine, tt=tt,
                          heads_per_block=heads_per_block),
        grid=(batch, _LRU_HEADS // heads_per_block, nt),
        in_specs=in_specs,
        out_specs=pl.BlockSpec((tt, cbw), row_map),
        out_shape=jax.ShapeDtypeStruct((rows, r), _BF16 if combine else _F32),
        scratch_shapes=scratch,
        compiler_params=pltpu.CompilerParams(
            dimension_semantics=("parallel", "parallel", "arbitrary"),
            vmem_limit_bytes=_vmem_limit(est)),
        name="rglru_bwd" if reverse else "rglru_fwd",
    )(*args)


def _conf_conv_kernel(v_ref, w_ref, b_ref, o_ref, z, pt, *, n, cb, n_col_blocks, lines_per_step):
    slabs = cb // _V7X_LANES
    pitch = n + _V7X_SUBLANES
    c = pl.program_id(1)
    zeros = jnp.zeros((_CONF_PAD * n, _V7X_LANES), _F32)

    def conv(s, write_line_block):
        def step(jj, carry):
            l0 = jj * lines_per_step
            accs = [jnp.broadcast_to(b_ref[:, s * _V7X_LANES:(s + 1) * _V7X_LANES], (n, _V7X_LANES))] * lines_per_step
            for m in range(lines_per_step + _CONF_K - 1):
                line = z[s, pl.ds(pl.multiple_of((l0 + m) * n, n), n), :]
                for l in range(lines_per_step):
                    k = m - l
                    if 0 <= k < _CONF_K:
                        accs[l] = accs[l] + w_ref[k:k + 1, s * _V7X_LANES:(s + 1) * _V7X_LANES] * line
            write_line_block(l0, accs)
            return carry
        lax.fori_loop(0, n // lines_per_step, step, 0)

    for s in range(slabs):
        lanes = slice(s * _V7X_LANES, (s + 1) * _V7X_LANES)
        z[s, 0:_CONF_PAD * n, :] = zeros
        z[s, (_CONF_PAD + n) * n:, :] = zeros

        @pl.when(c < n_col_blocks)
        def _():
            for i in range(n):
                pt[s, i * pitch:i * pitch + n, :] = v_ref[0, i * n:(i + 1) * n, lanes]
            for j in range(n):
                for i0 in range(n // _V7X_SUBLANES):
                    r0 = (_CONF_PAD + j) * n + i0 * _V7X_SUBLANES
                    z[s, r0:r0 + _V7X_SUBLANES, :] = pt[
                        s, pl.ds(i0 * _V7X_SUBLANES * pitch + j, _V7X_SUBLANES, stride=pitch), :]

            def write(l0, accs):
                for l in range(lines_per_step):
                    pt[s, pl.ds(pl.multiple_of((l0 + l) * pitch, _V7X_SUBLANES), n), :] = accs[l]
            conv(s, write)

            rows_per_store = 2 * _V7X_SUBLANES
            for i in range(n):
                for j0 in range(0, n, rows_per_store):
                    halves = [pt[s, pl.ds((j0 + h) * pitch + i, _V7X_SUBLANES, stride=pitch), :]
                              for h in range(0, rows_per_store, _V7X_SUBLANES)]
                    r0 = i * n + j0
                    o_ref[0, r0:r0 + rows_per_store, lanes] = jnp.concatenate(halves, axis=0).astype(o_ref.dtype)

        @pl.when(c >= n_col_blocks)
        def _():
            z[s, _CONF_PAD * n:(_CONF_PAD + n) * n, :] = v_ref[0, :, lanes]

            def write(l0, accs):
                for l in range(lines_per_step):
                    o_ref[0, pl.ds(pl.multiple_of((l0 + l) * n, n), n), lanes] = accs[l].astype(o_ref.dtype)
            conv(s, write)


def _conf_conv(v3d, w, b, *, n, cb=256, lines_per_step=4):
    bsz, t, ch = v3d.shape
    assert t == n * n and (ch // 2) % cb == 0
    n_col_blocks = (ch // 2) // cb
    slabs = cb // _V7X_LANES
    z_shape = (slabs, (n + 2 * _CONF_PAD) * n, _V7X_LANES)
    pt_shape = (slabs, n * (n + _V7X_SUBLANES), _V7X_LANES)
    est = 4 * _nbytes((t, cb), _F32) + _nbytes(z_shape, _F32) + _nbytes(pt_shape, _F32)
    return pl.pallas_call(
        functools.partial(_conf_conv_kernel, n=n, cb=cb, n_col_blocks=n_col_blocks,
                          lines_per_step=lines_per_step),
        grid=(bsz, ch // cb),
        in_specs=[pl.BlockSpec((1, t, cb), lambda i, c: (i, 0, c)),
                  pl.BlockSpec((_CONF_K, cb), lambda i, c: (0, c)),
                  pl.BlockSpec((1, cb), lambda i, c: (0, c))],
        out_specs=pl.BlockSpec((1, t, cb), lambda i, c: (i, 0, c)),
        out_shape=jax.ShapeDtypeStruct((bsz, t, ch), _F32),
        scratch_shapes=[pltpu.VMEM(z_shape, _F32), pltpu.VMEM(pt_shape, _F32)],
        compiler_params=pltpu.CompilerParams(dimension_semantics=("parallel", "parallel"),
                                             vmem_limit_bytes=_vmem_limit(est)),
        name="conformer_dwconv",
    )(v3d, w, b)


def _ln_silu_kernel(x_ref, g_ref, b_ref, o_ref):
    x = x_ref[...].astype(_F32)
    mu = jnp.mean(x, axis=-1, keepdims=True)
    xc = x - mu
    y = xc * lax.rsqrt(jnp.mean(xc * xc, axis=-1, keepdims=True) + _EPS)
    y = y * g_ref[...] + b_ref[...]
    o_ref[...] = (y * _sigmoid(y)).astype(o_ref.dtype)


def _ln_silu(x, g, b, tt=512):
    m, d = x.shape
    est = 2 * _nbytes((tt, d), x.dtype) + 2 * _nbytes((tt, d), _BF16) + 4 * _nbytes((tt, d), _F32)
    return pl.pallas_call(
        _ln_silu_kernel,
        grid=(m // tt,),
        in_specs=[pl.BlockSpec((tt, d), lambda i: (i, 0)),
                  pl.BlockSpec((1, d), lambda i: (0, 0)),
                  pl.BlockSpec((1, d), lambda i: (0, 0))],
        out_specs=pl.BlockSpec((tt, d), lambda i: (i, 0)),
        out_shape=jax.ShapeDtypeStruct((m, d), _BF16),
        compiler_params=pltpu.CompilerParams(dimension_semantics=("parallel",),
                                             vmem_limit_bytes=_vmem_limit(est)),
        name="layernorm_swish",
    )(x, g, b)


def kernel(x, c, ctx, c_ctx, w_ada, b_ada, norm1_g, w_in, lru_conv_w, lru_conv_b, lru_w_a, lru_b_a,
           lru_w_i, lru_b_i, lru_lam, w_lru_out, conf_dw_w, conf_dw_b, conf_ln_g, conf_ln_b,
           w_conf_out, w_o, norm2_g, w_ff1, w_ff2, final_g):
    assert w_ada.shape[0] == 1, "single-layer stack only (the context stream is then never updated)"
    bsz, t, d = x.shape
    tc = ctx.shape[1]
    r = lru_lam.shape[-1]
    cw = conf_dw_w.shape[-1]
    row = lambda v: v.reshape(1, -1)

    cc = jnp.concatenate([c, c_ctx[None, :], jnp.zeros((_V7X_SUBLANES - bsz - 1, d), _F32)], axis=0)
    mod = _ada(cc, w_ada[0], row(b_ada[0]))
    mod_l = mod[:bsz].reshape(bsz, 6, 1, d)
    sh1, sc1, g1, sh2, sc2, g2 = (mod_l[:, k] for k in range(6))
    mod_c = jnp.broadcast_to(mod[bsz].reshape(6, 1, 1, d), (6, bsz, 1, d))
    csh1, csc1 = mod_c[0], mod_c[1]

    u = _norm_mod(x, row(norm1_g[0]), sc1, sh1).reshape(bsz * t, d)
    uc = _norm_mod(ctx, row(norm1_g[0]), csc1, csh1).reshape(bsz * tc, d)

    w_in0 = w_in[0]
    gate_act, w_o_b = _proj(u, w_in0, offs=(0,), n_cols=r, epilogue="gelu", out_dtype=_BF16, tn=512,
                            side=w_o[0], name="in_proj_gate")
    x_br, w_lru_b = _proj(u, w_in0, offs=(r,), n_cols=r, epilogue="none", out_dtype=_F32, tn=512,
                          side=w_lru_out[0], name="in_proj_lru")
    v, w_conf_b = _proj(u, w_in0, offs=(2 * r, 2 * r + cw), n_cols=cw, epilogue="glu", out_dtype=_F32, tn=256,
                        side=w_conf_out[0], name="in_proj_glu")
    br_gates = _proj(u, w_in0, offs=(2 * r + 2 * cw,), n_cols=2 * d, epilogue="sigmoid", out_dtype=_BF16,
                     tn=512, name="in_proj_branch")
    xc_br = _proj(uc, w_in0, offs=(r,), n_cols=r, epilogue="none", out_dtype=_F32, tn=512, name="in_proj_ctx")

    def lru_params(dirn):
        return (lru_conv_w[0, dirn], row(lru_conv_b[0, dirn]), lru_w_a[0, dirn].astype(_BF16),
                row(lru_b_a[0, dirn]), lru_w_i[0, dirn].astype(_BF16), row(lru_b_i[0, dirn]),
                row(lru_lam[0, dirn]))

    zero_state = jnp.zeros((bsz, 1, r), _F32)
    hc_f = _lru(xc_br, *lru_params(0), zero_state, batch=bsz, reverse=False)
    hc_b = _lru(xc_br, *lru_params(1), zero_state, batch=bsz, reverse=True)
    h0_f = hc_f.reshape(bsz, tc, r)[:, tc - 1:tc, :]
    h0_b = hc_b.reshape(bsz, tc, r)[:, 0:1, :]
    h_f = _lru(x_br, *lru_params(0), h0_f, batch=bsz, reverse=False)
    lru_act = _lru(x_br, *lru_params(1), h0_b, batch=bsz, reverse=True, hf=h_f, gate=gate_act)

    conv = _conf_conv(v.reshape(bsz, t, cw), conf_dw_w[0], row(conf_dw_b[0]), n=_GRID_W)
    conf_act = _ln_silu(conv.reshape(bsz * t, cw), row(conf_ln_g[0]), row(conf_ln_b[0]))

    merged = _merge_mm(lru_act, conf_act, w_lru_b, w_conf_b, br_gates)
    x1, w_ff1_b = _mm(merged, w_o_b, epilogue="resid", resid=x.reshape(bsz * t, d), gate=g1,
                      rows_per_batch=t, tn=512, side=w_ff1[0], name="out_proj_residual")

    u2 = _norm_mod(x1.reshape(bsz, t, d), row(norm2_g[0]), sc2, sh2).reshape(bsz * t, d)
    hid, w_ff2_b = _mm(u2, w_ff1_b, epilogue="relu2", out_dtype=_BF16, side=w_ff2[0], name="mlp_up")
    x2 = _mm(hid, w_ff2_b, epilogue="resid", resid=x1, gate=g2, rows_per_batch=t, tk=4096,
             name="mlp_down_residual")
    return _final_norm(x2, row(final_g)).reshape(bsz, t, d)
```

```python
import functools

import jax
import jax.numpy as jnp
from jax import lax
from jax.experimental import pallas as pl
from jax.experimental.pallas import tpu as pltpu

_F32 = jnp.float32
_BF16 = jnp.bfloat16

_GRID_W = 64
_LRU_HEADS = 16
_LRU_CONV = 4
_LRU_C = 8.0
_CONF_K = 31
_CONF_PAD = (_CONF_K - 1) // 2
_EPS = 1e-6
_LOG2E = 1.4426950408889634

_V7X_SUBLANES = 8
_V7X_LANES = 128
_V7X_SCOPED_VMEM_BYTES = 60000 * 1024


_COMPILER_SCRATCH_BYTES = 2 << 20


def _vmem_limit(nbytes):
    return int(min(nbytes + _COMPILER_SCRATCH_BYTES, _V7X_SCOPED_VMEM_BYTES))


def _nbytes(shape, dtype):
    n = 1
    for s in shape:
        n *= s
    return n * jnp.dtype(dtype).itemsize


def _sigmoid(x):
    return 0.5 * jnp.tanh(0.5 * x) + 0.5


def _ada_kernel(cc_ref, w_ref, b_ref, o_ref):
    cc = cc_ref[...]
    s = cc * _sigmoid(cc)
    o_ref[...] = jnp.dot(s.astype(_BF16), w_ref[...].astype(_BF16),
                         preferred_element_type=_F32) + b_ref[...]


def _ada(cc, w, b, tn=512):
    m, d = cc.shape
    n = w.shape[1]
    est = 2 * _nbytes((d, tn), _F32) + _nbytes((d, tn), _BF16) + 4 * _nbytes((m, d), _F32) + (4 << 20)
    return pl.pallas_call(
        _ada_kernel,
        grid=(n // tn,),
        in_specs=[pl.BlockSpec((m, d), lambda j: (0, 0)),
                  pl.BlockSpec((d, tn), lambda j: (0, j)),
                  pl.BlockSpec((1, tn), lambda j: (0, j))],
        out_specs=pl.BlockSpec((m, tn), lambda j: (0, j)),
        out_shape=jax.ShapeDtypeStruct((m, n), _F32),
        compiler_params=pltpu.CompilerParams(dimension_semantics=("parallel",),
                                             vmem_limit_bytes=_vmem_limit(est)),
        name="ada_ln",
    )(cc, w, b)


def _norm_mod_kernel(x_ref, g_ref, sc_ref, sh_ref, o_ref):
    x = x_ref[0]
    ms = jnp.mean(x * x, axis=-1, keepdims=True)
    y = x * lax.rsqrt(ms + _EPS) * g_ref[...]
    o_ref[0] = (y * (1.0 + sc_ref[0]) + sh_ref[0]).astype(o_ref.dtype)


def _norm_mod(x, g, sc, sh, tt=512):
    b, t, d = x.shape
    tt = min(tt, t)
    est = 2 * _nbytes((tt, d), _F32) + 2 * _nbytes((tt, d), _BF16) + 2 * _nbytes((tt, d), _F32)
    return pl.pallas_call(
        _norm_mod_kernel,
        grid=(b, t // tt),
        in_specs=[pl.BlockSpec((1, tt, d), lambda i, j: (i, j, 0)),
                  pl.BlockSpec((1, d), lambda i, j: (0, 0)),
                  pl.BlockSpec((1, 1, d), lambda i, j: (i, 0, 0)),
                  pl.BlockSpec((1, 1, d), lambda i, j: (i, 0, 0))],
        out_specs=pl.BlockSpec((1, tt, d), lambda i, j: (i, j, 0)),
        out_shape=jax.ShapeDtypeStruct((b, t, d), _BF16),
        compiler_params=pltpu.CompilerParams(dimension_semantics=("parallel", "parallel"),
                                             vmem_limit_bytes=_vmem_limit(est)),
        name="rmsnorm_modulate",
    )(x, g, sc, sh)


def _final_norm_kernel(x_ref, g_ref, o_ref):
    x = x_ref[...]
    ms = jnp.mean(x * x, axis=-1, keepdims=True)
    o_ref[...] = x * lax.rsqrt(ms + _EPS) * g_ref[...]


def _final_norm(x, g, tt=512):
    m, d = x.shape
    est = 4 * _nbytes((tt, d), _F32) + 2 * _nbytes((tt, d), _F32)
    return pl.pallas_call(
        _final_norm_kernel,
        grid=(m // tt,),
        in_specs=[pl.BlockSpec((tt, d), lambda i: (i, 0)),
                  pl.BlockSpec((1, d), lambda i: (0, 0))],
        out_specs=pl.BlockSpec((tt, d), lambda i: (i, 0)),
        out_shape=jax.ShapeDtypeStruct((m, d), _F32),
        compiler_params=pltpu.CompilerParams(dimension_semantics=("parallel",),
                                             vmem_limit_bytes=_vmem_limit(est)),
        name="final_rmsnorm",
    )(x, g)


def _side_cast_body(body, n_in, n_side, *refs):
    side_ins = refs[n_in:n_in + n_side]
    o_ref = refs[n_in + n_side]
    side_outs = refs[n_in + n_side + 1:n_in + 2 * n_side + 1]
    for s_in, s_out in zip(side_ins, side_outs):
        s_out[...] = s_in[...].astype(_BF16)
    body(*refs[:n_in], o_ref, *refs[n_in + 2 * n_side + 1:])


def _matmul_call(body, *, grid, in_specs, out_spec, out_shape, scratch, semantics, est, name, args, sides=()):
    if not sides:
        return pl.pallas_call(
            body, grid=grid, in_specs=in_specs, out_specs=out_spec, out_shape=out_shape, scratch_shapes=scratch,
            compiler_params=pltpu.CompilerParams(dimension_semantics=semantics, vmem_limit_bytes=_vmem_limit(est)),
            name=name)(*args)
    n_steps = 1
    for g in grid:
        n_steps *= g

    def linear_step(idx):
        lin = 0
        for g, ix in zip(grid, idx):
            lin = lin * g + ix
        return lin

    side_args, side_in_specs, side_out_specs, side_shapes = [], [], [], []
    for side in sides:
        arr, col0, cols = side if isinstance(side, tuple) else (side, 0, side.shape[1])
        rows = arr.shape[0]
        rb = rows // n_steps
        assert rb * n_steps == rows and rb % (2 * _V7X_SUBLANES) == 0 and col0 % cols == 0, (arr.shape, grid)
        side_args.append(arr)
        side_in_specs.append(pl.BlockSpec((rb, cols), functools.partial(
            lambda *idx, cb: (linear_step(idx), cb), cb=col0 // cols)))
        side_out_specs.append(pl.BlockSpec((rb, cols), lambda *idx: (linear_step(idx), 0)))
        side_shapes.append(jax.ShapeDtypeStruct((rows, cols), _BF16))
        est += 2 * _nbytes((rb, cols), _F32) + 2 * _nbytes((rb, cols), _BF16)
    return pl.pallas_call(
        functools.partial(_side_cast_body, body, len(in_specs), len(sides)),
        grid=grid, in_specs=in_specs + side_in_specs, out_specs=[out_spec] + side_out_specs,
        out_shape=[out_shape] + side_shapes, scratch_shapes=scratch,
        compiler_params=pltpu.CompilerParams(dimension_semantics=semantics, vmem_limit_bytes=_vmem_limit(est)),
        name=name)(*args, *side_args)


def _epilogue(name, acc, x_ref=None, g_ref=None):
    if name == "none":
        return acc
    if name == "gelu":
        return jax.nn.gelu(acc)
    if name == "sigmoid":
        return _sigmoid(acc)
    if name == "relu2":
        return jnp.square(jnp.maximum(acc, 0.0))
    if name == "resid":
        return x_ref[...] + g_ref[0] * acc
    raise ValueError(name)


def _mm_kernel(a_ref, *rest, epilogue, nk, n_w):
    w_refs, rest = rest[:n_w], rest[n_w:]
    if epilogue == "resid":
        x_ref, g_ref, o_ref = rest[:3]
        scratch = rest[3:]
    else:
        x_ref = g_ref = None
        o_ref = rest[0]
        scratch = rest[1:]
    if nk == 1:
        a = a_ref[...]
        accs = [jnp.dot(a, w_ref[...], preferred_element_type=_F32) for w_ref in w_refs]
        if epilogue == "glu":
            r = accs[0] * _sigmoid(accs[1])
        else:
            r = _epilogue(epilogue, accs[0], x_ref, g_ref)
        o_ref[...] = r.astype(o_ref.dtype)
        return
    acc_ref = scratch[0] if scratch else o_ref
    k = pl.program_id(2)

    @pl.when(k == 0)
    def _():
        acc_ref[...] = jnp.zeros_like(acc_ref)

    acc_ref[...] += jnp.dot(a_ref[...], w_refs[0][...], preferred_element_type=_F32)

    @pl.when(k == nk - 1)
    def _():
        o_ref[...] = _epilogue(epilogue, acc_ref[...], x_ref, g_ref).astype(o_ref.dtype)


def _mm(a, w, *, n_off=0, n_cols=None, epilogue="none", out_dtype=_F32, resid=None, gate=None,
        rows_per_batch=None, glu_gate_off=None, tm=1024, tn=1024, tk=None, sides=(), name="matmul"):
    m, kdim = a.shape
    n_cols = w.shape[1] if n_cols is None else n_cols
    tk = kdim if tk is None else tk
    tm = min(tm, m)
    nk = kdim // tk
    col_offs = (n_off, glu_gate_off) if epilogue == "glu" else (n_off,)
    n_w = len(col_offs)
    assert n_w == 1 or nk == 1
    grid = (m // tm, n_cols // tn, nk)
    in_specs = [pl.BlockSpec((tm, tk), lambda i, j, k: (i, k))]
    in_specs += [pl.BlockSpec((tk, tn), functools.partial(lambda i, j, k, jo: (k, j + jo), jo=off // tn))
                 for off in col_offs]
    args = [a] + [w] * n_w
    est = 2 * _nbytes((tm, tk), a.dtype) + 2 * n_w * _nbytes((tk, tn), w.dtype)
    est += 2 * _nbytes((tm, tn), out_dtype) + (2 + n_w) * _nbytes((tm, tn), _F32)
    if epilogue == "resid":
        tiles_per_batch = rows_per_batch // tm
        in_specs += [pl.BlockSpec((tm, tn), lambda i, j, k: (i, j)),
                     pl.BlockSpec((1, 1, tn), lambda i, j, k: (i // tiles_per_batch, 0, j))]
        args += [resid, gate]
        est += 2 * _nbytes((tm, tn), _F32)
    scratch = []
    if nk > 1 and out_dtype != _F32:
        scratch = [pltpu.VMEM((tm, tn), _F32)]
        est += _nbytes((tm, tn), _F32)
    return _matmul_call(
        functools.partial(_mm_kernel, epilogue=epilogue, nk=nk, n_w=n_w), grid=grid, in_specs=in_specs,
        out_spec=pl.BlockSpec((tm, tn), lambda i, j, k: (i, j)),
        out_shape=jax.ShapeDtypeStruct((m, n_cols), out_dtype), scratch=scratch,
        semantics=("parallel", "parallel", "arbitrary"), est=est, name=name, args=args, sides=sides)


def _proj_kernel(a_ref, *rest, epilogue, n_w):
    w_refs, o_ref, wb_refs = rest[:n_w], rest[n_w], rest[n_w + 1:]

    @pl.when(pl.program_id(1) == 0)
    def _():
        for w_ref, wb in zip(w_refs, wb_refs):
            wb[...] = w_ref[...].astype(_BF16)

    a = a_ref[...]
    accs = [jnp.dot(a, wb[...], preferred_element_type=_F32) for wb in wb_refs]
    if epilogue == "glu":
        r = accs[0] * _sigmoid(accs[1])
    else:
        r = _epilogue(epilogue, accs[0])
    o_ref[...] = r.astype(o_ref.dtype)


def _proj(a, w, *, offs, n_cols, epilogue, out_dtype, tn, tm=1024, sides=(), name):
    m, kdim = a.shape
    tm = min(tm, m)
    n_w = len(offs)
    w_specs = [pl.BlockSpec((kdim, tn), functools.partial(lambda j, i, jo: (0, j + jo), jo=off // tn))
               for off in offs]
    est = 2 * _nbytes((tm, kdim), a.dtype) + n_w * (2 * _nbytes((kdim, tn), _F32) + _nbytes((kdim, tn), _BF16))
    est += 2 * _nbytes((tm, tn), out_dtype) + (n_w + 1) * _nbytes((tm, tn), _F32)
    return _matmul_call(
        functools.partial(_proj_kernel, epilogue=epilogue, n_w=n_w), grid=(n_cols // tn, m // tm),
        in_specs=[pl.BlockSpec((tm, kdim), lambda j, i: (i, 0))] + w_specs,
        out_spec=pl.BlockSpec((tm, tn), lambda j, i: (i, j)),
        out_shape=jax.ShapeDtypeStruct((m, n_cols), out_dtype),
        scratch=[pltpu.VMEM((kdim, tn), _BF16)] * n_w, semantics=("parallel", "arbitrary"), est=est,
        name=name, args=[a] + [w] * n_w, sides=sides)


def _merge_kernel(m_ref, cf_ref, wl_ref, wc_ref, gl_ref, gc_ref, o_ref, acc_ref, *, nk):
    k = pl.program_id(2)

    @pl.when(k == 0)
    def _():
        acc_ref[...] = jnp.zeros_like(acc_ref)

    lo = jnp.dot(m_ref[...], wl_ref[...], preferred_element_type=_F32)
    co = jnp.dot(cf_ref[...], wc_ref[...], preferred_element_type=_F32)
    acc_ref[...] += gl_ref[...].astype(_F32) * lo + gc_ref[...].astype(_F32) * co

    @pl.when(k == nk - 1)
    def _():
        o_ref[...] = acc_ref[...].astype(o_ref.dtype)


def _merge_mm(m_act, cf_act, w_lru, w_conf, gates, *, tm=1024, tn=1024, tk=2048, sides=()):
    m, kdim = m_act.shape
    n = w_lru.shape[1]
    nk = kdim // tk
    jc = n // tn
    est = 4 * _nbytes((tm, tk), _BF16) + 4 * _nbytes((tk, tn), _BF16) + 4 * _nbytes((tm, tn), _BF16)
    est += 2 * _nbytes((tm, tn), _BF16) + 4 * _nbytes((tm, tn), _F32)
    return _matmul_call(
        functools.partial(_merge_kernel, nk=nk), grid=(m // tm, n // tn, nk),
        in_specs=[pl.BlockSpec((tm, tk), lambda i, j, k: (i, k)),
                  pl.BlockSpec((tm, tk), lambda i, j, k: (i, k)),
                  pl.BlockSpec((tk, tn), lambda i, j, k: (k, j)),
                  pl.BlockSpec((tk, tn), lambda i, j, k: (k, j)),
                  pl.BlockSpec((tm, tn), lambda i, j, k: (i, j)),
                  pl.BlockSpec((tm, tn), lambda i, j, k: (i, j + jc))],
        out_spec=pl.BlockSpec((tm, tn), lambda i, j, k: (i, j)),
        out_shape=jax.ShapeDtypeStruct((m, n), _BF16), scratch=[pltpu.VMEM((tm, tn), _F32)],
        semantics=("parallel", "parallel", "arbitrary"), est=est, name="gated_merge",
        args=[m_act, cf_act, w_lru, w_conf, gates, gates], sides=sides)


def _lru_kernel(x_ref, cw_ref, cb_ref, wa_ref, ba_ref, wi_ref, bi_ref, lam_ref, h0_ref, *rest,
                reverse, combine, tt, heads_per_block):
    if combine:
        hf_ref, gate_ref, o_ref, px, xcp, xedge, carry = rest
    else:
        o_ref, px, xcp, xedge, carry = rest
    cbw = x_ref.shape[1]
    slabs = cbw // _V7X_LANES
    sub = _V7X_SUBLANES
    ln = tt // sub
    pitch = ln + sub
    nhalo = _LRU_CONV - 1
    base = 0 if reverse else nhalo * sub
    i = pl.program_id(2)
    srow = lax.broadcasted_iota(jnp.int32, (sub, _V7X_LANES), 0)

    @pl.when(i == 0)
    def _():
        carry[...] = h0_ref[0]
        xedge[...] = jnp.zeros_like(xedge)

    for s in range(slabs):
        lanes = slice(s * _V7X_LANES, (s + 1) * _V7X_LANES)
        for c in range(sub):
            px[s, c * pitch:c * pitch + ln, :] = x_ref[c * ln:(c + 1) * ln, lanes]
        for j in range(ln):
            xcp[s, base + j * sub:base + (j + 1) * sub, :] = px[s, pl.ds(j, sub, stride=pitch), :]
        for m in range(nhalo):
            prev = xedge[s, m * sub:(m + 1) * sub, :]
            if reverse:
                cur = xcp[s, m * sub:(m + 1) * sub, :]
                xedge[s, m * sub:(m + 1) * sub, :] = cur
                xcp[s, (ln + m) * sub:(ln + m + 1) * sub, :] = pltpu.roll(jnp.where(srow == 0, prev, cur), sub - 1, 0)
            else:
                cur = xcp[s, base + (ln - nhalo + m) * sub:base + (ln - nhalo + m + 1) * sub, :]
                xedge[s, m * sub:(m + 1) * sub, :] = cur
                xcp[s, m * sub:(m + 1) * sub, :] = pltpu.roll(jnp.where(srow == sub - 1, prev, cur), 1, 0)

    def conv_slab(s):
        lanes = slice(s * _V7X_LANES, (s + 1) * _V7X_LANES)
        y = jnp.broadcast_to(cb_ref[:, lanes], (tt, _V7X_LANES))
        for k in range(_LRU_CONV):
            off = (nhalo - k) if reverse else k
            y = y + cw_ref[k:k + 1, lanes] * xcp[s, off * sub:off * sub + tt, :]
        return y

    lam = lam_ref[...]
    z = -lam
    softplus = jnp.maximum(z, 0.0) + jnp.log1p(jnp.exp(-jnp.abs(z)))
    c8 = (-_LRU_C * _LOG2E) * softplus
    hw = cbw // heads_per_block
    order = range(ln - 1, -1, -1) if reverse else range(ln)
    chunk_order = range(sub - 1, -1, -1) if reverse else range(sub)
    for hh in range(heads_per_block):
        cs = slice(hh * hw, (hh + 1) * hw)
        head_slabs = range(hh * hw // _V7X_LANES, (hh + 1) * hw // _V7X_LANES)
        yh = jnp.concatenate([conv_slab(s) for s in head_slabs], axis=1)
        yb = yh.astype(_BF16)
        tr = jnp.tanh(jnp.dot(yb, wa_ref[hh] * 0.5, preferred_element_type=_F32) + 0.5 * ba_ref[:, cs])
        tg = jnp.tanh(jnp.dot(yb, wi_ref[hh] * 0.5, preferred_element_type=_F32) + 0.5 * bi_ref[:, cs])
        c8h = 0.5 * c8[:, cs]
        a = jnp.exp2(c8h * tr + c8h)
        v = 1.0 - a * a
        b = jnp.where(v > 0.0, v * lax.rsqrt(v), 0.0) * (0.5 * tg + 0.5) * yh

        h = jnp.zeros((sub, hw), _F32)
        p = jnp.ones((sub, hw), _F32)
        hs, ps = [None] * ln, [None] * ln
        for j in order:
            aj = a[j * sub:(j + 1) * sub]
            h = aj * h + b[j * sub:(j + 1) * sub]
            p = aj * p
            hs[j], ps[j] = h, p
        init = carry[:, cs]
        inits = [None] * sub
        for c in chunk_order:
            inits[c] = init
            init = p[c:c + 1, :] * init + h[c:c + 1, :]
        carry[:, cs] = init
        init_all = jnp.concatenate(inits, axis=0)
        for j in range(ln):
            hj = ps[j] * init_all + hs[j]
            for n, s in enumerate(head_slabs):
                px[s, pl.ds(j, sub, stride=pitch), :] = hj[:, n * _V7X_LANES:(n + 1) * _V7X_LANES]
        for s in head_slabs:
            lanes = slice(s * _V7X_LANES, (s + 1) * _V7X_LANES)
            for c in range(sub):
                rows = slice(c * ln, (c + 1) * ln)
                hn = px[s, c * pitch:c * pitch + ln, :]
                if combine:
                    hn = (hn + hf_ref[rows, lanes]) * gate_ref[rows, lanes].astype(_F32)
                o_ref[rows, lanes] = hn.astype(o_ref.dtype)


def _lru(x2d, cw, cb, wa, ba, wi, bi, lam, h0, *, batch, reverse, hf=None, gate=None,
         tt=512, heads_per_block=4):
    rows, r = x2d.shape
    t = rows // batch
    tt = min(tt, t)
    nt = t // tt
    hw = r // _LRU_HEADS
    cbw = hw * heads_per_block
    slabs = cbw // _V7X_LANES
    ln = tt // _V7X_SUBLANES
    combine = hf is not None

    def row_map(b, h, i):
        ti = (nt - 1 - i) if reverse else i
        return (b * nt + ti, h)

    chan = lambda b, h, i: (0, h)
    in_specs = [pl.BlockSpec((tt, cbw), row_map),
                pl.BlockSpec((_LRU_CONV, cbw), chan),
                pl.BlockSpec((1, cbw), chan),
                pl.BlockSpec((heads_per_block, hw, hw), lambda b, h, i: (h, 0, 0)),
                pl.BlockSpec((1, cbw), chan),
                pl.BlockSpec((heads_per_block, hw, hw), lambda b, h, i: (h, 0, 0)),
                pl.BlockSpec((1, cbw), chan),
                pl.BlockSpec((1, cbw), chan),
                pl.BlockSpec((1, 1, cbw), lambda b, h, i: (b, 0, h))]
    args = [x2d, cw, cb, wa, ba, wi, bi, lam, h0]
    if combine:
        in_specs += [pl.BlockSpec((tt, cbw), row_map), pl.BlockSpec((tt, cbw), row_map)]
        args += [hf, gate]
    px_shape = (slabs, _V7X_SUBLANES * (ln + _V7X_SUBLANES), _V7X_LANES)
    xcp_shape = (slabs, (ln + _LRU_CONV - 1) * _V7X_SUBLANES, _V7X_LANES)
    edge_shape = (slabs, (_LRU_CONV - 1) * _V7X_SUBLANES, _V7X_LANES)
    scratch = [pltpu.VMEM(px_shape, _F32), pltpu.VMEM(xcp_shape, _F32), pltpu.VMEM(edge_shape, _F32),
               pltpu.VMEM((1, cbw), _F32)]
    tile = _nbytes((tt, cbw), _F32)
    est = 8 * tile + _nbytes(px_shape, _F32) + _nbytes(xcp_shape, _F32) + 8 * tile
    return pl.pallas_call(
        functools.partial(_lru_kernel, reverse=reverse, combine=combine, tt=tt,
                          heads_per_block=heads_per_block),
        grid=(batch, _LRU_HEADS // heads_per_block, nt),
        in_specs=in_specs,
        out_specs=pl.BlockSpec((tt, cbw), row_map),
        out_shape=jax.ShapeDtypeStruct((rows, r), _BF16 if combine else _F32),
        scratch_shapes=scratch,
        compiler_params=pltpu.CompilerParams(
            dimension_semantics=("parallel", "parallel", "arbitrary"),
            vmem_limit_bytes=_vmem_limit(est)),
        name="rglru_bwd" if reverse else "rglru_fwd",
    )(*args)


def _conf_conv_kernel(v_ref, w_ref, b_ref, o_ref, z, pt, *, n, cb, n_col_blocks, lines_per_step):
    slabs = cb // _V7X_LANES
    pitch = n + _V7X_SUBLANES
    c = pl.program_id(1)
    zeros = jnp.zeros((_CONF_PAD * n, _V7X_LANES), _F32)

    def conv(s, write_line_block):
        def step(jj, carry):
            l0 = jj * lines_per_step
            accs = [jnp.broadcast_to(b_ref[:, s * _V7X_LANES:(s + 1) * _V7X_LANES], (n, _V7X_LANES))] * lines_per_step
            for m in range(lines_per_step + _CONF_K - 1):
                line = z[s, pl.ds(pl.multiple_of((l0 + m) * n, n), n), :]
                for l in range(lines_per_step):
                    k = m - l
                    if 0 <= k < _CONF_K:
                        accs[l] = accs[l] + w_ref[k:k + 1, s * _V7X_LANES:(s + 1) * _V7X_LANES] * line
            write_line_block(l0, accs)
            return carry
        lax.fori_loop(0, n // lines_per_step, step, 0)

    for s in range(slabs):
        lanes = slice(s * _V7X_LANES, (s + 1) * _V7X_LANES)
        z[s, 0:_CONF_PAD * n, :] = zeros
        z[s, (_CONF_PAD + n) * n:, :] = zeros

        @pl.when(c < n_col_blocks)
        def _():
            for i in range(n):
                pt[s, i * pitch:i * pitch + n, :] = v_ref[0, i * n:(i + 1) * n, lanes]
            for j in range(n):
                for i0 in range(n // _V7X_SUBLANES):
                    r0 = (_CONF_PAD + j) * n + i0 * _V7X_SUBLANES
                    z[s, r0:r0 + _V7X_SUBLANES, :] = pt[
                        s, pl.ds(i0 * _V7X_SUBLANES * pitch + j, _V7X_SUBLANES, stride=pitch), :]

            def write(l0, accs):
                for l in range(lines_per_step):
                    pt[s, pl.ds(pl.multiple_of((l0 + l) * pitch, _V7X_SUBLANES), n), :] = accs[l]
            conv(s, write)

            rows_per_store = 2 * _V7X_SUBLANES
            for i in range(n):
                for j0 in range(0, n, rows_per_store):
                    halves = [pt[s, pl.ds((j0 + h) * pitch + i, _V7X_SUBLANES, stride=pitch), :]
                              for h in range(0, rows_per_store, _V7X_SUBLANES)]
                    r0 = i * n + j0
                    o_ref[0, r0:r0 + rows_per_store, lanes] = jnp.concatenate(halves, axis=0).astype(o_ref.dtype)

        @pl.when(c >= n_col_blocks)
        def _():
            z[s, _CONF_PAD * n:(_CONF_PAD + n) * n, :] = v_ref[0, :, lanes]

            def write(l0, accs):
                for l in range(lines_per_step):
                    o_ref[0, pl.ds(pl.multiple_of((l0 + l) * n, n), n), lanes] = accs[l].astype(o_ref.dtype)
            conv(s, write)


def _conf_conv(v3d, w, b, *, n, cb=256, lines_per_step=4):
    bsz, t, ch = v3d.shape
    assert t == n * n and (ch // 2) % cb == 0
    n_col_blocks = (ch // 2) // cb
    slabs = cb // _V7X_LANES
    z_shape = (slabs, (n + 2 * _CONF_PAD) * n, _V7X_LANES)
    pt_shape = (slabs, n * (n + _V7X_SUBLANES), _V7X_LANES)
    est = 4 * _nbytes((t, cb), _F32) + _nbytes(z_shape, _F32) + _nbytes(pt_shape, _F32)
    return pl.pallas_call(
        functools.partial(_conf_conv_kernel, n=n, cb=cb, n_col_blocks=n_col_blocks,
                          lines_per_step=lines_per_step),
        grid=(bsz, ch // cb),
        in_specs=[pl.BlockSpec((1, t, cb), lambda i, c: (i, 0, c)),
                  pl.BlockSpec((_CONF_K, cb), lambda i, c: (0, c)),
                  pl.BlockSpec((1, cb), lambda i, c: (0, c))],
        out_specs=pl.BlockSpec((1, t, cb), lambda i, c: (i, 0, c)),
        out_shape=jax.ShapeDtypeStruct((bsz, t, ch), _F32),
        scratch_shapes=[pltpu.VMEM(z_shape, _F32), pltpu.VMEM(pt_shape, _F32)],
        compiler_params=pltpu.CompilerParams(dimension_semantics=("parallel", "parallel"),
                                             vmem_limit_bytes=_vmem_limit(est)),
        name="conformer_dwconv",
    )(v3d, w, b)


def _ln_silu_kernel(x_ref, g_ref, b_ref, o_ref):
    x = x_ref[...].astype(_F32)
    mu = jnp.mean(x, axis=-1, keepdims=True)
    xc = x - mu
    y = xc * lax.rsqrt(jnp.mean(xc * xc, axis=-1, keepdims=True) + _EPS)
    y = y * g_ref[...] + b_ref[...]
    o_ref[...] = (y * jax.nn.sigmoid(y)).astype(o_ref.dtype)


def _ln_silu(x, g, b, tt=512):
    m, d = x.shape
    est = 2 * _nbytes((tt, d), x.dtype) + 2 * _nbytes((tt, d), _BF16) + 4 * _nbytes((tt, d), _F32)
    return pl.pallas_call(
        _ln_silu_kernel,
        grid=(m // tt,),
        in_specs=[pl.BlockSpec((tt, d), lambda i: (i, 0)),
                  pl.BlockSpec((1, d), lambda i: (0, 0)),
                  pl.BlockSpec((1, d), lambda i: (0, 0))],
        out_specs=pl.BlockSpec((tt, d), lambda i: (i, 0)),
        out_shape=jax.ShapeDtypeStruct((m, d), _BF16),
        compiler_params=pltpu.CompilerParams(dimension_semantics=("parallel",),
                                             vmem_limit_bytes=_vmem_limit(est)),
        name="layernorm_swish",
    )(x, g, b)


def kernel(x, c, ctx, c_ctx, w_ada, b_ada, norm1_g, w_in, lru_conv_w, lru_conv_b, lru_w_a, lru_b_a,
           lru_w_i, lru_b_i, lru_lam, w_lru_out, conf_dw_w, conf_dw_b, conf_ln_g, conf_ln_b,
           w_conf_out, w_o, norm2_g, w_ff1, w_ff2, final_g):
    assert w_ada.shape[0] == 1, "single-layer stack only (the context stream is then never updated)"
    bsz, t, d = x.shape
    tc = ctx.shape[1]
    r = lru_lam.shape[-1]
    cw = conf_dw_w.shape[-1]
    row = lambda v: v.reshape(1, -1)

    cc = jnp.concatenate([c, c_ctx[None, :], jnp.zeros((_V7X_SUBLANES - bsz - 1, d), _F32)], axis=0)
    mod = _ada(cc, w_ada[0], row(b_ada[0]))
    mod_l = mod[:bsz].reshape(bsz, 6, 1, d)
    sh1, sc1, g1, sh2, sc2, g2 = (mod_l[:, k] for k in range(6))
    mod_c = jnp.broadcast_to(mod[bsz].reshape(6, 1, 1, d), (6, bsz, 1, d))
    csh1, csc1 = mod_c[0], mod_c[1]

    u = _norm_mod(x, row(norm1_g[0]), sc1, sh1).reshape(bsz * t, d)
    uc = _norm_mod(ctx, row(norm1_g[0]), csc1, csh1).reshape(bsz * tc, d)

    w_in0 = w_in[0]
    x_br, w_gate_b = _proj(u, w_in0, offs=(r,), n_cols=r, epilogue="none", out_dtype=_F32, tn=512,
                           sides=[(w_in0, 0, r)], name="in_proj_lru")
    gate_act, w_glu_b = _mm(u, w_gate_b, epilogue="gelu", out_dtype=_BF16,
                            sides=[(w_in0, 2 * r, 2 * cw)], name="in_proj_gate")
    v, w_branch_b = _mm(u, w_glu_b, n_cols=cw, epilogue="glu", glu_gate_off=cw, out_dtype=_F32, tn=512,
                        sides=[(w_in0, 2 * r + 2 * cw, 2 * d)], name="in_proj_glu")
    br_gates, w_lru_b, w_conf_b = _mm(u, w_branch_b, epilogue="sigmoid", out_dtype=_BF16,
                                      sides=[w_lru_out[0], w_conf_out[0]], name="in_proj_branch")
    xc_br = _proj(uc, w_in0, offs=(r,), n_cols=r, epilogue="none", out_dtype=_F32, tn=512, name="in_proj_ctx")

    def lru_params(dirn):
        return (lru_conv_w[0, dirn], row(lru_conv_b[0, dirn]), lru_w_a[0, dirn].astype(_BF16),
                row(lru_b_a[0, dirn]), lru_w_i[0, dirn].astype(_BF16), row(lru_b_i[0, dirn]),
                row(lru_lam[0, dirn]))

    zero_state = jnp.zeros((bsz, 1, r), _F32)
    hc_f = _lru(xc_br, *lru_params(0), zero_state, batch=bsz, reverse=False)
    hc_b = _lru(xc_br, *lru_params(1), zero_state, batch=bsz, reverse=True)
    h0_f = hc_f.reshape(bsz, tc, r)[:, tc - 1:tc, :]
    h0_b = hc_b.reshape(bsz, tc, r)[:, 0:1, :]
    h_f = _lru(x_br, *lru_params(0), h0_f, batch=bsz, reverse=False)
    lru_act = _lru(x_br, *lru_params(1), h0_b, batch=bsz, reverse=True, hf=h_f, gate=gate_act)

    conv = _conf_conv(v.reshape(bsz, t, cw), conf_dw_w[0], row(conf_dw_b[0]), n=_GRID_W)
    conf_act = _ln_silu(conv.reshape(bsz * t, cw), row(conf_ln_g[0]), row(conf_ln_b[0]))

    merged, w_o_b = _merge_mm(lru_act, conf_act, w_lru_b, w_conf_b, br_gates, sides=[w_o[0]])
    x1, w_ff1_b = _mm(merged, w_o_b, epilogue="resid", resid=x.reshape(bsz * t, d), gate=g1,
                      rows_per_batch=t, tn=512, sides=[w_ff1[0]], name="out_proj_residual")

    u2 = _norm_mod(x1.reshape(bsz, t, d), row(norm2_g[0]), sc2, sh2).reshape(bsz * t, d)
    hid, w_ff2_b = _mm(u2, w_ff1_b, epilogue="relu2", out_dtype=_BF16, sides=[w_ff2[0]], name="mlp_up")
    x2 = _mm(hid, w_ff2_b, epilogue="resid", resid=x1, gate=g2, rows_per_batch=t, tk=4096,
             name="mlp_down_residual")
    return _final_norm(x2, row(final_g)).reshape(bsz, t, d)
```

```python
import functools

import jax
import jax.numpy as jnp
from jax import lax
from jax.experimental import pallas as pl
from jax.experimental.pallas import tpu as pltpu

_F32 = jnp.float32
_BF16 = jnp.bfloat16

_GRID_W = 64
_LRU_HEADS = 16
_LRU_CONV = 4
_LRU_C = 8.0
_CONF_K = 31
_CONF_PAD = (_CONF_K - 1) // 2
_EPS = 1e-6
_LOG2E = 1.4426950408889634

_V7X_SUBLANES = 8
_V7X_LANES = 128
_V7X_SCOPED_VMEM_BYTES = 60000 * 1024


_COMPILER_SCRATCH_BYTES = 2 << 20


def _vmem_limit(nbytes):
    return int(min(nbytes + _COMPILER_SCRATCH_BYTES, _V7X_SCOPED_VMEM_BYTES))


def _nbytes(shape, dtype):
    n = 1
    for s in shape:
        n *= s
    return n * jnp.dtype(dtype).itemsize


def _sigmoid(x):
    return 0.5 * jnp.tanh(0.5 * x) + 0.5


def _ada_kernel(cc_ref, w_ref, b_ref, o_ref):
    cc = cc_ref[...]
    s = cc * _sigmoid(cc)
    o_ref[...] = jnp.dot(s.astype(_BF16), w_ref[...].astype(_BF16),
                         preferred_element_type=_F32) + b_ref[...]


def _ada(cc, w, b, tn=512):
    m, d = cc.shape
    n = w.shape[1]
    est = 2 * _nbytes((d, tn), _F32) + 2 * _nbytes((d, tn), _BF16) + 4 * _nbytes((m, d), _F32)
    return pl.pallas_call(
        _ada_kernel,
        grid=(n // tn,),
        in_specs=[pl.BlockSpec((m, d), lambda j: (0, 0)),
                  pl.BlockSpec((d, tn), lambda j: (0, j)),
                  pl.BlockSpec((1, tn), lambda j: (0, j))],
        out_specs=pl.BlockSpec((m, tn), lambda j: (0, j)),
        out_shape=jax.ShapeDtypeStruct((m, n), _F32),
        compiler_params=pltpu.CompilerParams(dimension_semantics=("parallel",),
                                             vmem_limit_bytes=_vmem_limit(est)),
        name="ada_ln",
    )(cc, w, b)


def _norm_mod_kernel(x_ref, g_ref, sc_ref, sh_ref, o_ref):
    x = x_ref[0]
    ms = jnp.mean(x * x, axis=-1, keepdims=True)
    y = x * lax.rsqrt(ms + _EPS) * g_ref[...]
    o_ref[0] = (y * (1.0 + sc_ref[0]) + sh_ref[0]).astype(o_ref.dtype)


def _norm_mod(x, g, sc, sh, tt=512):
    b, t, d = x.shape
    tt = min(tt, t)
    est = 2 * _nbytes((tt, d), _F32) + 2 * _nbytes((tt, d), _BF16) + 2 * _nbytes((tt, d), _F32)
    return pl.pallas_call(
        _norm_mod_kernel,
        grid=(b, t // tt),
        in_specs=[pl.BlockSpec((1, tt, d), lambda i, j: (i, j, 0)),
                  pl.BlockSpec((1, d), lambda i, j: (0, 0)),
                  pl.BlockSpec((1, 1, d), lambda i, j: (i, 0, 0)),
                  pl.BlockSpec((1, 1, d), lambda i, j: (i, 0, 0))],
        out_specs=pl.BlockSpec((1, tt, d), lambda i, j: (i, j, 0)),
        out_shape=jax.ShapeDtypeStruct((b, t, d), _BF16),
        compiler_params=pltpu.CompilerParams(dimension_semantics=("parallel", "parallel"),
                                             vmem_limit_bytes=_vmem_limit(est)),
        name="rmsnorm_modulate",
    )(x, g, sc, sh)


def _final_norm_kernel(x_ref, g_ref, o_ref):
    x = x_ref[...]
    ms = jnp.mean(x * x, axis=-1, keepdims=True)
    o_ref[...] = x * lax.rsqrt(ms + _EPS) * g_ref[...]


def _final_norm(x, g, tt=512):
    m, d = x.shape
    est = 4 * _nbytes((tt, d), _F32) + 2 * _nbytes((tt, d), _F32)
    return pl.pallas_call(
        _final_norm_kernel,
        grid=(m // tt,),
        in_specs=[pl.BlockSpec((tt, d), lambda i: (i, 0)),
                  pl.BlockSpec((1, d), lambda i: (0, 0))],
        out_specs=pl.BlockSpec((tt, d), lambda i: (i, 0)),
        out_shape=jax.ShapeDtypeStruct((m, d), _F32),
        compiler_params=pltpu.CompilerParams(dimension_semantics=("parallel",),
                                             vmem_limit_bytes=_vmem_limit(est)),
        name="final_rmsnorm",
    )(x, g)


def _side_cast_body(body, n_in, n_side, *refs):
    side_ins = refs[n_in:n_in + n_side]
    o_ref = refs[n_in + n_side]
    side_outs = refs[n_in + n_side + 1:n_in + 2 * n_side + 1]
    for s_in, s_out in zip(side_ins, side_outs):
        s_out[...] = s_in[...].astype(_BF16)
    body(*refs[:n_in], o_ref, *refs[n_in + 2 * n_side + 1:])


def _matmul_call(body, *, grid, in_specs, out_spec, out_shape, scratch, semantics, est, name, args, sides=()):
    if not sides:
        return pl.pallas_call(
            body, grid=grid, in_specs=in_specs, out_specs=out_spec, out_shape=out_shape, scratch_shapes=scratch,
            compiler_params=pltpu.CompilerParams(dimension_semantics=semantics, vmem_limit_bytes=_vmem_limit(est)),
            name=name)(*args)
    n_steps = 1
    for g in grid:
        n_steps *= g

    def linear_step(idx):
        lin = 0
        for g, ix in zip(grid, idx):
            lin = lin * g + ix
        return lin

    side_args, side_in_specs, side_out_specs, side_shapes = [], [], [], []
    for side in sides:
        arr, col0, cols = side if isinstance(side, tuple) else (side, 0, side.shape[1])
        rows = arr.shape[0]
        rb = rows // n_steps
        assert rb * n_steps == rows and rb % (2 * _V7X_SUBLANES) == 0 and col0 % cols == 0, (arr.shape, grid)
        side_args.append(arr)
        side_in_specs.append(pl.BlockSpec((rb, cols), functools.partial(
            lambda *idx, cb: (linear_step(idx), cb), cb=col0 // cols)))
        side_out_specs.append(pl.BlockSpec((rb, cols), lambda *idx: (linear_step(idx), 0)))
        side_shapes.append(jax.ShapeDtypeStruct((rows, cols), _BF16))
        est += 2 * _nbytes((rb, cols), _F32) + 2 * _nbytes((rb, cols), _BF16)
    return pl.pallas_call(
        functools.partial(_side_cast_body, body, len(in_specs), len(sides)),
        grid=grid, in_specs=in_specs + side_in_specs, out_specs=[out_spec] + side_out_specs,
        out_shape=[out_shape] + side_shapes, scratch_shapes=scratch,
        compiler_params=pltpu.CompilerParams(dimension_semantics=semantics, vmem_limit_bytes=_vmem_limit(est)),
        name=name)(*args, *side_args)


def _epilogue(name, acc, x_ref=None, g_ref=None):
    if name == "none":
        return acc
    if name == "gelu":
        return jax.nn.gelu(acc)
    if name == "sigmoid":
        return _sigmoid(acc)
    if name == "relu2":
        return jnp.square(jnp.maximum(acc, 0.0))
    if name == "resid":
        return x_ref[...] + g_ref[0] * acc
    raise ValueError(name)


def _mm_kernel(a_ref, *rest, epilogue, nk, n_w):
    w_refs, rest = rest[:n_w], rest[n_w:]
    if epilogue == "resid":
        x_ref, g_ref, o_ref = rest[:3]
        scratch = rest[3:]
    else:
        x_ref = g_ref = None
        o_ref = rest[0]
        scratch = rest[1:]
    if nk == 1:
        a = a_ref[...]
        accs = [jnp.dot(a, w_ref[...], preferred_element_type=_F32) for w_ref in w_refs]
        if epilogue == "glu":
            r = accs[0] * _sigmoid(accs[1])
        else:
            r = _epilogue(epilogue, accs[0], x_ref, g_ref)
        o_ref[...] = r.astype(o_ref.dtype)
        return
    acc_ref = scratch[0] if scratch else o_ref
    k = pl.program_id(2)

    @pl.when(k == 0)
    def _():
        acc_ref[...] = jnp.zeros_like(acc_ref)

    acc_ref[...] += jnp.dot(a_ref[...], w_refs[0][...], preferred_element_type=_F32)

    @pl.when(k == nk - 1)
    def _():
        o_ref[...] = _epilogue(epilogue, acc_ref[...], x_ref, g_ref).astype(o_ref.dtype)


def _mm(a, w, *, n_off=0, n_cols=None, epilogue="none", out_dtype=_F32, resid=None, gate=None,
        rows_per_batch=None, glu_gate_off=None, tm=1024, tn=1024, tk=None, sides=(), name="matmul"):
    m, kdim = a.shape
    n_cols = w.shape[1] if n_cols is None else n_cols
    tk = kdim if tk is None else tk
    tm = min(tm, m)
    nk = kdim // tk
    col_offs = (n_off, glu_gate_off) if epilogue == "glu" else (n_off,)
    n_w = len(col_offs)
    assert n_w == 1 or nk == 1
    grid = (m // tm, n_cols // tn, nk)
    in_specs = [pl.BlockSpec((tm, tk), lambda i, j, k: (i, k))]
    in_specs += [pl.BlockSpec((tk, tn), functools.partial(lambda i, j, k, jo: (k, j + jo), jo=off // tn))
                 for off in col_offs]
    args = [a] + [w] * n_w
    est = 2 * _nbytes((tm, tk), a.dtype) + 2 * n_w * _nbytes((tk, tn), w.dtype)
    est += 2 * _nbytes((tm, tn), out_dtype) + (2 + n_w) * _nbytes((tm, tn), _F32)
    if epilogue == "resid":
        tiles_per_batch = rows_per_batch // tm
        in_specs += [pl.BlockSpec((tm, tn), lambda i, j, k: (i, j)),
                     pl.BlockSpec((1, 1, tn), lambda i, j, k: (i // tiles_per_batch, 0, j))]
        args += [resid, gate]
        est += 2 * _nbytes((tm, tn), _F32)
    scratch = []
    if nk > 1 and out_dtype != _F32:
        scratch = [pltpu.VMEM((tm, tn), _F32)]
        est += _nbytes((tm, tn), _F32)
    return _matmul_call(
        functools.partial(_mm_kernel, epilogue=epilogue, nk=nk, n_w=n_w), grid=grid, in_specs=in_specs,
        out_spec=pl.BlockSpec((tm, tn), lambda i, j, k: (i, j)),
        out_shape=jax.ShapeDtypeStruct((m, n_cols), out_dtype), scratch=scratch,
        semantics=("parallel", "parallel", "arbitrary"), est=est, name=name, args=args, sides=sides)


def _proj_kernel(a_ref, *rest, epilogue, n_w):
    w_refs, o_ref, wb_refs = rest[:n_w], rest[n_w], rest[n_w + 1:]

    @pl.when(pl.program_id(1) == 0)
    def _():
        for w_ref, wb in zip(w_refs, wb_refs):
            wb[...] = w_ref[...].astype(_BF16)

    a = a_ref[...]
    accs = [jnp.dot(a, wb[...], preferred_element_type=_F32) for wb in wb_refs]
    if epilogue == "glu":
        r = accs[0] * _sigmoid(accs[1])
    else:
        r = _epilogue(epilogue, accs[0])
    o_ref[...] = r.astype(o_ref.dtype)


def _proj(a, w, *, offs, n_cols, epilogue, out_dtype, tn, tm=1024, sides=(), name):
    m, kdim = a.shape
    tm = min(tm, m)
    n_w = len(offs)
    w_specs = [pl.BlockSpec((kdim, tn), functools.partial(lambda j, i, jo: (0, j + jo), jo=off // tn))
               for off in offs]
    est = 2 * _nbytes((tm, kdim), a.dtype) + n_w * (2 * _nbytes((kdim, tn), _F32) + _nbytes((kdim, tn), _BF16))
    est += 2 * _nbytes((tm, tn), out_dtype) + (n_w + 1) * _nbytes((tm, tn), _F32)
    return _matmul_call(
        functools.partial(_proj_kernel, epilogue=epilogue, n_w=n_w), grid=(n_cols // tn, m // tm),
        in_specs=[pl.BlockSpec((tm, kdim), lambda j, i: (i, 0))] + w_specs,
        out_spec=pl.BlockSpec((tm, tn), lambda j, i: (i, j)),
        out_shape=jax.ShapeDtypeStruct((m, n_cols), out_dtype),
        scratch=[pltpu.VMEM((kdim, tn), _BF16)] * n_w, semantics=("parallel", "arbitrary"), est=est,
        name=name, args=[a] + [w] * n_w, sides=sides)


def _out_proj_norm_kernel(a_ref, w_ref, x_ref, g_ref, ng_ref, sc_ref, sh_ref, o_ref, u_ref):
    acc = jnp.dot(a_ref[...], w_ref[...], preferred_element_type=_F32)
    x1 = x_ref[...] + g_ref[0] * acc
    o_ref[...] = x1
    ms = jnp.mean(x1 * x1, axis=-1, keepdims=True)
    y = x1 * lax.rsqrt(ms + _EPS) * ng_ref[...]
    u_ref[...] = (y * (1.0 + sc_ref[0]) + sh_ref[0]).astype(u_ref.dtype)


def _out_proj_norm(a, w, resid, gate, norm_g, sc, sh, *, rows_per_batch, tm=256):
    m, kdim = a.shape
    n = w.shape[1]
    tiles_per_batch = rows_per_batch // tm
    per_batch = pl.BlockSpec((1, 1, n), lambda i: (i // tiles_per_batch, 0, 0))
    est = 2 * _nbytes((tm, kdim), a.dtype) + _nbytes((kdim, n), w.dtype) + 4 * _nbytes((tm, n), _F32)
    est += 2 * _nbytes((tm, n), _BF16) + 3 * _nbytes((tm, n), _F32)
    return pl.pallas_call(
        _out_proj_norm_kernel,
        grid=(m // tm,),
        in_specs=[pl.BlockSpec((tm, kdim), lambda i: (i, 0)),
                  pl.BlockSpec((kdim, n), lambda i: (0, 0), pipeline_mode=pl.Buffered(1)),
                  pl.BlockSpec((tm, n), lambda i: (i, 0)),
                  per_batch,
                  pl.BlockSpec((1, n), lambda i: (0, 0)),
                  per_batch,
                  per_batch],
        out_specs=[pl.BlockSpec((tm, n), lambda i: (i, 0)), pl.BlockSpec((tm, n), lambda i: (i, 0))],
        out_shape=[jax.ShapeDtypeStruct((m, n), _F32), jax.ShapeDtypeStruct((m, n), _BF16)],
        compiler_params=pltpu.CompilerParams(dimension_semantics=("parallel",),
                                             vmem_limit_bytes=_vmem_limit(est)),
        name="out_proj_residual_norm",
    )(a, w, resid, gate, norm_g, sc, sh)


def _merge_kernel(m_ref, cf_ref, wl_ref, wc_ref, gl_ref, gc_ref, o_ref, acc_ref, *, nk):
    k = pl.program_id(2)

    @pl.when(k == 0)
    def _():
        acc_ref[...] = jnp.zeros_like(acc_ref)

    lo = jnp.dot(m_ref[...], wl_ref[...], preferred_element_type=_F32)
    co = jnp.dot(cf_ref[...], wc_ref[...], preferred_element_type=_F32)
    acc_ref[...] += gl_ref[...].astype(_F32) * lo + gc_ref[...].astype(_F32) * co

    @pl.when(k == nk - 1)
    def _():
        o_ref[...] = acc_ref[...].astype(o_ref.dtype)


def _merge_mm(m_act, cf_act, w_lru, w_conf, gates, *, tm=1024, tn=1024, tk=2048, sides=()):
    m, kdim = m_act.shape
    n = w_lru.shape[1]
    nk = kdim // tk
    jc = n // tn
    est = 4 * _nbytes((tm, tk), _BF16) + 4 * _nbytes((tk, tn), _BF16) + 4 * _nbytes((tm, tn), _BF16)
    est += 2 * _nbytes((tm, tn), _BF16) + 4 * _nbytes((tm, tn), _F32)
    return _matmul_call(
        functools.partial(_merge_kernel, nk=nk), grid=(m // tm, n // tn, nk),
        in_specs=[pl.BlockSpec((tm, tk), lambda i, j, k: (i, k)),
                  pl.BlockSpec((tm, tk), lambda i, j, k: (i, k)),
                  pl.BlockSpec((tk, tn), lambda i, j, k: (k, j)),
                  pl.BlockSpec((tk, tn), lambda i, j, k: (k, j)),
                  pl.BlockSpec((tm, tn), lambda i, j, k: (i, j)),
                  pl.BlockSpec((tm, tn), lambda i, j, k: (i, j + jc))],
        out_spec=pl.BlockSpec((tm, tn), lambda i, j, k: (i, j)),
        out_shape=jax.ShapeDtypeStruct((m, n), _BF16), scratch=[pltpu.VMEM((tm, tn), _F32)],
        semantics=("parallel", "parallel", "arbitrary"), est=est, name="gated_merge",
        args=[m_act, cf_act, w_lru, w_conf, gates, gates], sides=sides)


def _lru_kernel(x_ref, cw_ref, cb_ref, wa_ref, ba_ref, wi_ref, bi_ref, lam_ref, h0_ref, *rest,
                reverse, combine, tt, heads_per_block):
    if combine:
        hf_ref, gate_ref, o_ref, px, xcp, xedge, carry = rest
    else:
        o_ref, px, xcp, xedge, carry = rest
    cbw = x_ref.shape[1]
    slabs = cbw // _V7X_LANES
    sub = _V7X_SUBLANES
    ln = tt // sub
    pitch = ln + sub
    nhalo = _LRU_CONV - 1
    base = 0 if reverse else nhalo * sub
    i = pl.program_id(2)
    srow = lax.broadcasted_iota(jnp.int32, (sub, _V7X_LANES), 0)

    @pl.when(i == 0)
    def _():
        carry[...] = h0_ref[0]
        xedge[...] = jnp.zeros_like(xedge)

    for s in range(slabs):
        lanes = slice(s * _V7X_LANES, (s + 1) * _V7X_LANES)
        for c in range(sub):
            px[s, c * pitch:c * pitch + ln, :] = x_ref[c * ln:(c + 1) * ln, lanes]
        for j in range(ln):
            xcp[s, base + j * sub:base + (j + 1) * sub, :] = px[s, pl.ds(j, sub, stride=pitch), :]
        for m in range(nhalo):
            prev = xedge[s, m * sub:(m + 1) * sub, :]
            if reverse:
                cur = xcp[s, m * sub:(m + 1) * sub, :]
                xedge[s, m * sub:(m + 1) * sub, :] = cur
                xcp[s, (ln + m) * sub:(ln + m + 1) * sub, :] = pltpu.roll(jnp.where(srow == 0, prev, cur), sub - 1, 0)
            else:
                cur = xcp[s, base + (ln - nhalo + m) * sub:base + (ln - nhalo + m + 1) * sub, :]
                xedge[s, m * sub:(m + 1) * sub, :] = cur
                xcp[s, m * sub:(m + 1) * sub, :] = pltpu.roll(jnp.where(srow == sub - 1, prev, cur), 1, 0)

    def conv_slab(s):
        lanes = slice(s * _V7X_LANES, (s + 1) * _V7X_LANES)
        y = jnp.broadcast_to(cb_ref[:, lanes], (tt, _V7X_LANES))
        for k in range(_LRU_CONV):
            off = (nhalo - k) if reverse else k
            y = y + cw_ref[k:k + 1, lanes] * xcp[s, off * sub:off * sub + tt, :]
        return y

    lam = lam_ref[...]
    z = -lam
    softplus = jnp.maximum(z, 0.0) + jnp.log1p(jnp.exp(-jnp.abs(z)))
    c8 = (-_LRU_C * _LOG2E) * softplus
    hw = cbw // heads_per_block
    order = range(ln - 1, -1, -1) if reverse else range(ln)
    chunk_order = range(sub - 1, -1, -1) if reverse else range(sub)
    for hh in range(heads_per_block):
        cs = slice(hh * hw, (hh + 1) * hw)
        head_slabs = range(hh * hw // _V7X_LANES, (hh + 1) * hw // _V7X_LANES)
        yh = jnp.concatenate([conv_slab(s) for s in head_slabs], axis=1)
        yb = yh.astype(_BF16)
        tr = jnp.tanh(jnp.dot(yb, wa_ref[hh] * 0.5, preferred_element_type=_F32) + 0.5 * ba_ref[:, cs])
        tg = jnp.tanh(jnp.dot(yb, wi_ref[hh] * 0.5, preferred_element_type=_F32) + 0.5 * bi_ref[:, cs])
        c8h = 0.5 * c8[:, cs]
        a = jnp.exp2(c8h * tr + c8h)
        v = 1.0 - a * a
        b = jnp.where(v > 0.0, v * lax.rsqrt(v), 0.0) * (0.5 * tg + 0.5) * yh

        h = jnp.zeros((sub, hw), _F32)
        p = jnp.ones((sub, hw), _F32)
        hs, ps = [None] * ln, [None] * ln
        for j in order:
            aj = a[j * sub:(j + 1) * sub]
            h = aj * h + b[j * sub:(j + 1) * sub]
            p = aj * p
            hs[j], ps[j] = h, p
        init = carry[:, cs]
        inits = [None] * sub
        for c in chunk_order:
            inits[c] = init
            init = p[c:c + 1, :] * init + h[c:c + 1, :]
        carry[:, cs] = init
        init_all = jnp.concatenate(inits, axis=0)
        for j in range(ln):
            hj = ps[j] * init_all + hs[j]
            for n, s in enumerate(head_slabs):
                px[s, pl.ds(j, sub, stride=pitch), :] = hj[:, n * _V7X_LANES:(n + 1) * _V7X_LANES]
        for s in head_slabs:
            lanes = slice(s * _V7X_LANES, (s + 1) * _V7X_LANES)
            for c in range(sub):
                rows = slice(c * ln, (c + 1) * ln)
                hn = px[s, c * pitch:c * pitch + ln, :]
                if combine:
                    hn = (hn + hf_ref[rows, lanes]) * gate_ref[rows, lanes].astype(_F32)
                o_ref[rows, lanes] = hn.astype(o_ref.dtype)


def _lru(x2d, cw, cb, wa, ba, wi, bi, lam, h0, *, batch, reverse, hf=None, gate=None,
         tt=512, heads_per_block=4):
    rows, r = x2d.shape
    t = rows // batch
    tt = min(tt, t)
    nt = t // tt
    hw = r // _LRU_HEADS
    cbw = hw * heads_per_block
    slabs = cbw // _V7X_LANES
    ln = tt // _V7X_SUBLANES
    combine = hf is not None

    def row_map(b, h, i):
        ti = (nt - 1 - i) if reverse else i
        return (b * nt + ti, h)

    chan = lambda b, h, i: (0, h)
    in_specs = [pl.BlockSpec((tt, cbw), row_map),
                pl.BlockSpec((_LRU_CONV, cbw), chan),
                pl.BlockSpec((1, cbw), chan),
                pl.BlockSpec((heads_per_block, hw, hw), lambda b, h, i: (h, 0, 0)),
                pl.BlockSpec((1, cbw), chan),
                pl.BlockSpec((heads_per_block, hw, hw), lambda b, h, i: (h, 0, 0)),
                pl.BlockSpec((1, cbw), chan),
                pl.BlockSpec((1, cbw), chan),
                pl.BlockSpec((1, 1, cbw), lambda b, h, i: (b, 0, h))]
    args = [x2d, cw, cb, wa, ba, wi, bi, lam, h0]
    if combine:
        in_specs += [pl.BlockSpec((tt, cbw), row_map), pl.BlockSpec((tt, cbw), row_map)]
        args += [hf, gate]
    px_shape = (slabs, _V7X_SUBLANES * (ln + _V7X_SUBLANES), _V7X_LANES)
    xcp_shape = (slabs, (ln + _LRU_CONV - 1) * _V7X_SUBLANES, _V7X_LANES)
    edge_shape = (slabs, (_LRU_CONV - 1) * _V7X_SUBLANES, _V7X_LANES)
    scratch = [pltpu.VMEM(px_shape, _F32), pltpu.VMEM(xcp_shape, _F32), pltpu.VMEM(edge_shape, _F32),
               pltpu.VMEM((1, cbw), _F32)]
    tile = _nbytes((tt, cbw), _F32)
    est = 8 * tile + _nbytes(px_shape, _F32) + _nbytes(xcp_shape, _F32) + 8 * tile
    return pl.pallas_call(
        functools.partial(_lru_kernel, reverse=reverse, combine=combine, tt=tt,
                          heads_per_block=heads_per_block),
        grid=(batch, _LRU_HEADS // heads_per_block, nt),
        in_specs=in_specs,
        out_specs=pl.BlockSpec((tt, cbw), row_map),
        out_shape=jax.ShapeDtypeStruct((rows, r), _BF16 if combine else _F32),
        scratch_shapes=scratch,
        compiler_params=pltpu.CompilerParams(
            dimension_semantics=("parallel", "parallel", "arbitrary"),
            vmem_limit_bytes=_vmem_limit(est)),
        name="rglru_bwd" if reverse else "rglru_fwd",
    )(*args)


def _conf_conv_kernel(v_ref, w_ref, b_ref, o_ref, z, pt, *, n, cb, n_col_blocks, lines_per_step):
    slabs = cb // _V7X_LANES
    pitch = n + _V7X_SUBLANES
    c = pl.program_id(1)
    zeros = jnp.zeros((_CONF_PAD * n, _V7X_LANES), _F32)

    def conv(s, write_line_block):
        def step(jj, carry):
            l0 = jj * lines_per_step
            accs = [jnp.broadcast_to(b_ref[:, s * _V7X_LANES:(s + 1) * _V7X_LANES], (n, _V7X_LANES))] * lines_per_step
            for m in range(lines_per_step + _CONF_K - 1):
                line = z[s, pl.ds(pl.multiple_of((l0 + m) * n, n), n), :]
                for l in range(lines_per_step):
                    k = m - l
                    if 0 <= k < _CONF_K:
                        accs[l] = accs[l] + w_ref[k:k + 1, s * _V7X_LANES:(s + 1) * _V7X_LANES] * line
            write_line_block(l0, accs)
            return carry
        lax.fori_loop(0, n // lines_per_step, step, 0)

    for s in range(slabs):
        lanes = slice(s * _V7X_LANES, (s + 1) * _V7X_LANES)
        z[s, 0:_CONF_PAD * n, :] = zeros
        z[s, (_CONF_PAD + n) * n:, :] = zeros

        @pl.when(c < n_col_blocks)
        def _():
            for i in range(n):
                pt[s, i * pitch:i * pitch + n, :] = v_ref[0, i * n:(i + 1) * n, lanes]
            for j in range(n):
                for i0 in range(n // _V7X_SUBLANES):
                    r0 = (_CONF_PAD + j) * n + i0 * _V7X_SUBLANES
                    z[s, r0:r0 + _V7X_SUBLANES, :] = pt[
                        s, pl.ds(i0 * _V7X_SUBLANES * pitch + j, _V7X_SUBLANES, stride=pitch), :]

            def write(l0, accs):
                for l in range(lines_per_step):
                    pt[s, pl.ds(pl.multiple_of((l0 + l) * pitch, _V7X_SUBLANES), n), :] = accs[l]
            conv(s, write)

            rows_per_store = 2 * _V7X_SUBLANES
            for i in range(n):
                for j0 in range(0, n, rows_per_store):
                    halves = [pt[s, pl.ds((j0 + h) * pitch + i, _V7X_SUBLANES, stride=pitch), :]
                              for h in range(0, rows_per_store, _V7X_SUBLANES)]
                    r0 = i * n + j0
                    o_ref[0, r0:r0 + rows_per_store, lanes] = jnp.concatenate(halves, axis=0).astype(o_ref.dtype)

        @pl.when(c >= n_col_blocks)
        def _():
            z[s, _CONF_PAD * n:(_CONF_PAD + n) * n, :] = v_ref[0, :, lanes]

            def write(l0, accs):
                for l in range(lines_per_step):
                    o_ref[0, pl.ds(pl.multiple_of((l0 + l) * n, n), n), lanes] = accs[l].astype(o_ref.dtype)
            conv(s, write)


def _conf_conv(v3d, w, b, *, n, cb=256, lines_per_step=4):
    bsz, t, ch = v3d.shape
    assert t == n * n and (ch // 2) % cb == 0
    n_col_blocks = (ch // 2) // cb
    slabs = cb // _V7X_LANES
    z_shape = (slabs, (n + 2 * _CONF_PAD) * n, _V7X_LANES)
    pt_shape = (slabs, n * (n + _V7X_SUBLANES), _V7X_LANES)
    est = 4 * _nbytes((t, cb), _F32) + _nbytes(z_shape, _F32) + _nbytes(pt_shape, _F32)
    return pl.pallas_call(
        functools.partial(_conf_conv_kernel, n=n, cb=cb, n_col_blocks=n_col_blocks,
                          lines_per_step=lines_per_step),
        grid=(bsz, ch // cb),
        in_specs=[pl.BlockSpec((1, t, cb), lambda i, c: (i, 0, c)),
                  pl.BlockSpec((_CONF_K, cb), lambda i, c: (0, c)),
                  pl.BlockSpec((1, cb), lambda i, c: (0, c))],
        out_specs=pl.BlockSpec((1, t, cb), lambda i, c: (i, 0, c)),
        out_shape=jax.ShapeDtypeStruct((bsz, t, ch), _F32),
        scratch_shapes=[pltpu.VMEM(z_shape, _F32), pltpu.VMEM(pt_shape, _F32)],
        compiler_params=pltpu.CompilerParams(dimension_semantics=("parallel", "parallel"),
                                             vmem_limit_bytes=_vmem_limit(est)),
        name="conformer_dwconv",
    )(v3d, w, b)


def _ln_silu_kernel(x_ref, g_ref, b_ref, o_ref):
    x = x_ref[...].astype(_F32)
    mu = jnp.mean(x, axis=-1, keepdims=True)
    xc = x - mu
    y = xc * lax.rsqrt(jnp.mean(xc * xc, axis=-1, keepdims=True) + _EPS)
    y = y * g_ref[...] + b_ref[...]
    o_ref[...] = (y * jax.nn.sigmoid(y)).astype(o_ref.dtype)


def _ln_silu(x, g, b, tt=512):
    m, d = x.shape
    est = 2 * _nbytes((tt, d), x.dtype) + 2 * _nbytes((tt, d), _BF16) + 4 * _nbytes((tt, d), _F32)
    return pl.pallas_call(
        _ln_silu_kernel,
        grid=(m // tt,),
        in_specs=[pl.BlockSpec((tt, d), lambda i: (i, 0)),
                  pl.BlockSpec((1, d), lambda i: (0, 0)),
                  pl.BlockSpec((1, d), lambda i: (0, 0))],
        out_specs=pl.BlockSpec((tt, d), lambda i: (i, 0)),
        out_shape=jax.ShapeDtypeStruct((m, d), _BF16),
        compiler_params=pltpu.CompilerParams(dimension_semantics=("parallel",),
                                             vmem_limit_bytes=_vmem_limit(est)),
        name="layernorm_swish",
    )(x, g, b)


def kernel(x, c, ctx, c_ctx, w_ada, b_ada, norm1_g, w_in, lru_conv_w, lru_conv_b, lru_w_a, lru_b_a,
           lru_w_i, lru_b_i, lru_lam, w_lru_out, conf_dw_w, conf_dw_b, conf_ln_g, conf_ln_b,
           w_conf_out, w_o, norm2_g, w_ff1, w_ff2, final_g):
    assert w_ada.shape[0] == 1, "single-layer stack only (the context stream is then never updated)"
    bsz, t, d = x.shape
    tc = ctx.shape[1]
    r = lru_lam.shape[-1]
    cw = conf_dw_w.shape[-1]
    row = lambda v: v.reshape(1, -1)

    cc = jnp.concatenate([c, c_ctx[None, :], jnp.zeros((_V7X_SUBLANES - bsz - 1, d), _F32)], axis=0)
    mod = _ada(cc, w_ada[0], row(b_ada[0]))
    mod_l = mod[:bsz].reshape(bsz, 6, 1, d)
    sh1, sc1, g1, sh2, sc2, g2 = (mod_l[:, k] for k in range(6))
    mod_c = jnp.broadcast_to(mod[bsz].reshape(6, 1, 1, d), (6, bsz, 1, d))
    csh1, csc1 = mod_c[0], mod_c[1]

    u = _norm_mod(x, row(norm1_g[0]), sc1, sh1).reshape(bsz * t, d)
    uc = _norm_mod(ctx, row(norm1_g[0]), csc1, csh1).reshape(bsz * tc, d)

    w_in0 = w_in[0]
    x_br, w_gate_b = _proj(u, w_in0, offs=(r,), n_cols=r, epilogue="none", out_dtype=_F32, tn=512,
                           sides=[(w_in0, 0, r)], name="in_proj_lru")
    gate_act, w_glu_b = _mm(u, w_gate_b, epilogue="gelu", out_dtype=_BF16,
                            sides=[(w_in0, 2 * r, 2 * cw)], name="in_proj_gate")
    v, w_branch_b = _mm(u, w_glu_b, n_cols=cw, epilogue="glu", glu_gate_off=cw, out_dtype=_F32, tn=512,
                        sides=[(w_in0, 2 * r + 2 * cw, 2 * d)], name="in_proj_glu")
    br_gates, w_lru_b, w_conf_b, w_ff1_b = _mm(u, w_branch_b, epilogue="sigmoid", out_dtype=_BF16,
                                               sides=[w_lru_out[0], w_conf_out[0], w_ff1[0]],
                                               name="in_proj_branch")
    xc_br = _proj(uc, w_in0, offs=(r,), n_cols=r, epilogue="none", out_dtype=_F32, tn=512, name="in_proj_ctx")

    def lru_params(dirn):
        return (lru_conv_w[0, dirn], row(lru_conv_b[0, dirn]), lru_w_a[0, dirn].astype(_BF16),
                row(lru_b_a[0, dirn]), lru_w_i[0, dirn].astype(_BF16), row(lru_b_i[0, dirn]),
                row(lru_lam[0, dirn]))

    zero_state = jnp.zeros((bsz, 1, r), _F32)
    hc_f = _lru(xc_br, *lru_params(0), zero_state, batch=bsz, reverse=False)
    hc_b = _lru(xc_br, *lru_params(1), zero_state, batch=bsz, reverse=True)
    h0_f = hc_f.reshape(bsz, tc, r)[:, tc - 1:tc, :]
    h0_b = hc_b.reshape(bsz, tc, r)[:, 0:1, :]
    h_f = _lru(x_br, *lru_params(0), h0_f, batch=bsz, reverse=False)
    lru_act = _lru(x_br, *lru_params(1), h0_b, batch=bsz, reverse=True, hf=h_f, gate=gate_act)

    conv = _conf_conv(v.reshape(bsz, t, cw), conf_dw_w[0], row(conf_dw_b[0]), n=_GRID_W)
    conf_act = _ln_silu(conv.reshape(bsz * t, cw), row(conf_ln_g[0]), row(conf_ln_b[0]))

    merged, w_o_b = _merge_mm(lru_act, conf_act, w_lru_b, w_conf_b, br_gates, sides=[w_o[0]])
    x1, u2 = _out_proj_norm(merged, w_o_b, x.reshape(bsz * t, d), g1, row(norm2_g[0]), sc2, sh2,
                            rows_per_batch=t)
    hid, w_ff2_b = _mm(u2, w_ff1_b, epilogue="relu2", out_dtype=_BF16, sides=[w_ff2[0]], name="mlp_up")
    x2 = _mm(hid, w_ff2_b, epilogue="resid", resid=x1, gate=g2, rows_per_batch=t, tk=4096,
             name="mlp_down_residual")
    return _final_norm(x2, row(final_g)).reshape(bsz, t, d)
```

```python
import functools

import jax
import jax.numpy as jnp
from jax import lax
from jax.experimental import pallas as pl
from jax.experimental.pallas import tpu as pltpu

_F32 = jnp.float32
_BF16 = jnp.bfloat16

_GRID_W = 64
_LRU_HEADS = 16
_LRU_CONV = 4
_LRU_C = 8.0
_CONF_K = 31
_CONF_PAD = (_CONF_K - 1) // 2
_EPS = 1e-6
_LOG2E = 1.4426950408889634

_V7X_SUBLANES = 8
_V7X_LANES = 128
_V7X_SCOPED_VMEM_BYTES = 60000 * 1024


_COMPILER_SCRATCH_BYTES = 2 << 20


def _vmem_limit(nbytes):
    return int(min(nbytes + _COMPILER_SCRATCH_BYTES, _V7X_SCOPED_VMEM_BYTES))


def _nbytes(shape, dtype):
    n = 1
    for s in shape:
        n *= s
    return n * jnp.dtype(dtype).itemsize


def _sigmoid(x):
    return 0.5 * jnp.tanh(0.5 * x) + 0.5


def _ada_kernel(cc_ref, w_ref, b_ref, o_ref):
    cc = cc_ref[...]
    s = cc * _sigmoid(cc)
    o_ref[...] = jnp.dot(s.astype(_BF16), w_ref[...].astype(_BF16),
                         preferred_element_type=_F32) + b_ref[...]


def _ada(cc, w, b, tn=512):
    m, d = cc.shape
    n = w.shape[1]
    est = 2 * _nbytes((d, tn), _F32) + 2 * _nbytes((d, tn), _BF16) + 4 * _nbytes((m, d), _F32)
    return pl.pallas_call(
        _ada_kernel,
        grid=(n // tn,),
        in_specs=[pl.BlockSpec((m, d), lambda j: (0, 0)),
                  pl.BlockSpec((d, tn), lambda j: (0, j)),
                  pl.BlockSpec((1, tn), lambda j: (0, j))],
        out_specs=pl.BlockSpec((m, tn), lambda j: (0, j)),
        out_shape=jax.ShapeDtypeStruct((m, n), _F32),
        compiler_params=pltpu.CompilerParams(dimension_semantics=("parallel",),
                                             vmem_limit_bytes=_vmem_limit(est)),
        name="ada_ln",
    )(cc, w, b)


def _norm_mod_kernel(x_ref, g_ref, sc_ref, sh_ref, o_ref):
    x = x_ref[0]
    ms = jnp.mean(x * x, axis=-1, keepdims=True)
    y = x * lax.rsqrt(ms + _EPS) * g_ref[...]
    o_ref[0] = (y * (1.0 + sc_ref[0]) + sh_ref[0]).astype(o_ref.dtype)


def _norm_mod(x, g, sc, sh, tt=512):
    b, t, d = x.shape
    tt = min(tt, t)
    est = 2 * _nbytes((tt, d), _F32) + 2 * _nbytes((tt, d), _BF16) + 2 * _nbytes((tt, d), _F32)
    return pl.pallas_call(
        _norm_mod_kernel,
        grid=(b, t // tt),
        in_specs=[pl.BlockSpec((1, tt, d), lambda i, j: (i, j, 0)),
                  pl.BlockSpec((1, d), lambda i, j: (0, 0)),
                  pl.BlockSpec((1, 1, d), lambda i, j: (i, 0, 0)),
                  pl.BlockSpec((1, 1, d), lambda i, j: (i, 0, 0))],
        out_specs=pl.BlockSpec((1, tt, d), lambda i, j: (i, j, 0)),
        out_shape=jax.ShapeDtypeStruct((b, t, d), _BF16),
        compiler_params=pltpu.CompilerParams(dimension_semantics=("parallel", "parallel"),
                                             vmem_limit_bytes=_vmem_limit(est)),
        name="rmsnorm_modulate",
    )(x, g, sc, sh)


def _final_norm_kernel(x_ref, g_ref, o_ref):
    x = x_ref[...]
    ms = jnp.mean(x * x, axis=-1, keepdims=True)
    o_ref[...] = x * lax.rsqrt(ms + _EPS) * g_ref[...]


def _final_norm(x, g, tt=512):
    m, d = x.shape
    est = 4 * _nbytes((tt, d), _F32) + 2 * _nbytes((tt, d), _F32)
    return pl.pallas_call(
        _final_norm_kernel,
        grid=(m // tt,),
        in_specs=[pl.BlockSpec((tt, d), lambda i: (i, 0)),
                  pl.BlockSpec((1, d), lambda i: (0, 0))],
        out_specs=pl.BlockSpec((tt, d), lambda i: (i, 0)),
        out_shape=jax.ShapeDtypeStruct((m, d), _F32),
        compiler_params=pltpu.CompilerParams(dimension_semantics=("parallel",),
                                             vmem_limit_bytes=_vmem_limit(est)),
        name="final_rmsnorm",
    )(x, g)


def _side_cast_body(body, n_in, n_side, *refs):
    side_ins = refs[n_in:n_in + n_side]
    o_ref = refs[n_in + n_side]
    side_outs = refs[n_in + n_side + 1:n_in + 2 * n_side + 1]
    for s_in, s_out in zip(side_ins, side_outs):
        s_out[...] = s_in[...].astype(_BF16)
    body(*refs[:n_in], o_ref, *refs[n_in + 2 * n_side + 1:])


def _matmul_call(body, *, grid, in_specs, out_spec, out_shape, scratch, semantics, est, name, args, sides=()):
    if not sides:
        return pl.pallas_call(
            body, grid=grid, in_specs=in_specs, out_specs=out_spec, out_shape=out_shape, scratch_shapes=scratch,
            compiler_params=pltpu.CompilerParams(dimension_semantics=semantics, vmem_limit_bytes=_vmem_limit(est)),
            name=name)(*args)
    n_steps = 1
    for g in grid:
        n_steps *= g

    def linear_step(idx):
        lin = 0
        for g, ix in zip(grid, idx):
            lin = lin * g + ix
        return lin

    side_args, side_in_specs, side_out_specs, side_shapes = [], [], [], []
    for side in sides:
        arr, col0, cols = side if isinstance(side, tuple) else (side, 0, side.shape[1])
        rows = arr.shape[0]
        rb = rows // n_steps
        assert rb * n_steps == rows and rb % (2 * _V7X_SUBLANES) == 0 and col0 % cols == 0, (arr.shape, grid)
        side_args.append(arr)
        side_in_specs.append(pl.BlockSpec((rb, cols), functools.partial(
            lambda *idx, cb: (linear_step(idx), cb), cb=col0 // cols)))
        side_out_specs.append(pl.BlockSpec((rb, cols), lambda *idx: (linear_step(idx), 0)))
        side_shapes.append(jax.ShapeDtypeStruct((rows, cols), _BF16))
        est += 2 * _nbytes((rb, cols), _F32) + 2 * _nbytes((rb, cols), _BF16)
    return pl.pallas_call(
        functools.partial(_side_cast_body, body, len(in_specs), len(sides)),
        grid=grid, in_specs=in_specs + side_in_specs, out_specs=[out_spec] + side_out_specs,
        out_shape=[out_shape] + side_shapes, scratch_shapes=scratch,
        compiler_params=pltpu.CompilerParams(dimension_semantics=semantics, vmem_limit_bytes=_vmem_limit(est)),
        name=name)(*args, *side_args)


def _epilogue(name, acc, x_ref=None, g_ref=None):
    if name == "none":
        return acc
    if name == "gelu":
        return jax.nn.gelu(acc)
    if name == "sigmoid":
        return _sigmoid(acc)
    if name == "relu2":
        return jnp.square(jnp.maximum(acc, 0.0))
    if name == "resid":
        return x_ref[...] + g_ref[0] * acc
    raise ValueError(name)


def _mm_kernel(a_ref, *rest, epilogue, nk, n_w):
    w_refs, rest = rest[:n_w], rest[n_w:]
    if epilogue == "resid":
        x_ref, g_ref, o_ref = rest[:3]
        scratch = rest[3:]
    else:
        x_ref = g_ref = None
        o_ref = rest[0]
        scratch = rest[1:]
    if nk == 1:
        a = a_ref[...]
        accs = [jnp.dot(a, w_ref[...], preferred_element_type=_F32) for w_ref in w_refs]
        if epilogue == "glu":
            r = accs[0] * _sigmoid(accs[1])
        else:
            r = _epilogue(epilogue, accs[0], x_ref, g_ref)
        o_ref[...] = r.astype(o_ref.dtype)
        return
    acc_ref = scratch[0] if scratch else o_ref
    k = pl.program_id(2)

    @pl.when(k == 0)
    def _():
        acc_ref[...] = jnp.zeros_like(acc_ref)

    acc_ref[...] += jnp.dot(a_ref[...], w_refs[0][...], preferred_element_type=_F32)

    @pl.when(k == nk - 1)
    def _():
        o_ref[...] = _epilogue(epilogue, acc_ref[...], x_ref, g_ref).astype(o_ref.dtype)


def _mm(a, w, *, n_off=0, n_cols=None, epilogue="none", out_dtype=_F32, resid=None, gate=None,
        rows_per_batch=None, glu_gate_off=None, tm=1024, tn=1024, tk=None, sides=(), name="matmul"):
    m, kdim = a.shape
    n_cols = w.shape[1] if n_cols is None else n_cols
    tk = kdim if tk is None else tk
    tm = min(tm, m)
    nk = kdim // tk
    col_offs = (n_off, glu_gate_off) if epilogue == "glu" else (n_off,)
    n_w = len(col_offs)
    assert n_w == 1 or nk == 1
    grid = (m // tm, n_cols // tn, nk)
    in_specs = [pl.BlockSpec((tm, tk), lambda i, j, k: (i, k))]
    in_specs += [pl.BlockSpec((tk, tn), functools.partial(lambda i, j, k, jo: (k, j + jo), jo=off // tn))
                 for off in col_offs]
    args = [a] + [w] * n_w
    est = 2 * _nbytes((tm, tk), a.dtype) + 2 * n_w * _nbytes((tk, tn), w.dtype)
    est += 2 * _nbytes((tm, tn), out_dtype) + (2 + n_w) * _nbytes((tm, tn), _F32)
    if epilogue == "resid":
        tiles_per_batch = rows_per_batch // tm
        in_specs += [pl.BlockSpec((tm, tn), lambda i, j, k: (i, j)),
                     pl.BlockSpec((1, 1, tn), lambda i, j, k: (i // tiles_per_batch, 0, j))]
        args += [resid, gate]
        est += 2 * _nbytes((tm, tn), _F32)
    scratch = []
    if nk > 1 and out_dtype != _F32:
        scratch = [pltpu.VMEM((tm, tn), _F32)]
        est += _nbytes((tm, tn), _F32)
    return _matmul_call(
        functools.partial(_mm_kernel, epilogue=epilogue, nk=nk, n_w=n_w), grid=grid, in_specs=in_specs,
        out_spec=pl.BlockSpec((tm, tn), lambda i, j, k: (i, j)),
        out_shape=jax.ShapeDtypeStruct((m, n_cols), out_dtype), scratch=scratch,
        semantics=("parallel", "parallel", "arbitrary"), est=est, name=name, args=args, sides=sides)


def _proj_kernel(a_ref, *rest, epilogue, n_w):
    w_refs, o_ref, wb_refs = rest[:n_w], rest[n_w], rest[n_w + 1:]

    @pl.when(pl.program_id(1) == 0)
    def _():
        for w_ref, wb in zip(w_refs, wb_refs):
            wb[...] = w_ref[...].astype(_BF16)

    a = a_ref[...]
    accs = [jnp.dot(a, wb[...], preferred_element_type=_F32) for wb in wb_refs]
    if epilogue == "glu":
        r = accs[0] * _sigmoid(accs[1])
    else:
        r = _epilogue(epilogue, accs[0])
    o_ref[...] = r.astype(o_ref.dtype)


def _proj(a, w, *, offs, n_cols, epilogue, out_dtype, tn, tm=1024, sides=(), name):
    m, kdim = a.shape
    tm = min(tm, m)
    n_w = len(offs)
    w_specs = [pl.BlockSpec((kdim, tn), functools.partial(lambda j, i, jo: (0, j + jo), jo=off // tn))
               for off in offs]
    est = 2 * _nbytes((tm, kdim), a.dtype) + n_w * (2 * _nbytes((kdim, tn), _F32) + _nbytes((kdim, tn), _BF16))
    est += 2 * _nbytes((tm, tn), out_dtype) + (n_w + 1) * _nbytes((tm, tn), _F32)
    return _matmul_call(
        functools.partial(_proj_kernel, epilogue=epilogue, n_w=n_w), grid=(n_cols // tn, m // tm),
        in_specs=[pl.BlockSpec((tm, kdim), lambda j, i: (i, 0))] + w_specs,
        out_spec=pl.BlockSpec((tm, tn), lambda j, i: (i, j)),
        out_shape=jax.ShapeDtypeStruct((m, n_cols), out_dtype),
        scratch=[pltpu.VMEM((kdim, tn), _BF16)] * n_w, semantics=("parallel", "arbitrary"), est=est,
        name=name, args=[a] + [w] * n_w, sides=sides)


def _out_proj_norm_kernel(a_ref, w_ref, x_ref, g_ref, ng_ref, sc_ref, sh_ref, o_ref, u_ref):
    acc = jnp.dot(a_ref[...], w_ref[...], preferred_element_type=_F32)
    x1 = x_ref[...] + g_ref[0] * acc
    o_ref[...] = x1
    ms = jnp.mean(x1 * x1, axis=-1, keepdims=True)
    y = x1 * lax.rsqrt(ms + _EPS) * ng_ref[...]
    u_ref[...] = (y * (1.0 + sc_ref[0]) + sh_ref[0]).astype(u_ref.dtype)


def _out_proj_norm(a, w, resid, gate, norm_g, sc, sh, *, rows_per_batch, tm=256):
    m, kdim = a.shape
    n = w.shape[1]
    tiles_per_batch = rows_per_batch // tm
    per_batch = pl.BlockSpec((1, 1, n), lambda i: (i // tiles_per_batch, 0, 0))
    est = 2 * _nbytes((tm, kdim), a.dtype) + _nbytes((kdim, n), w.dtype) + 4 * _nbytes((tm, n), _F32)
    est += 2 * _nbytes((tm, n), _BF16) + 3 * _nbytes((tm, n), _F32)
    return pl.pallas_call(
        _out_proj_norm_kernel,
        grid=(m // tm,),
        in_specs=[pl.BlockSpec((tm, kdim), lambda i: (i, 0)),
                  pl.BlockSpec((kdim, n), lambda i: (0, 0), pipeline_mode=pl.Buffered(1)),
                  pl.BlockSpec((tm, n), lambda i: (i, 0)),
                  per_batch,
                  pl.BlockSpec((1, n), lambda i: (0, 0)),
                  per_batch,
                  per_batch],
        out_specs=[pl.BlockSpec((tm, n), lambda i: (i, 0)), pl.BlockSpec((tm, n), lambda i: (i, 0))],
        out_shape=[jax.ShapeDtypeStruct((m, n), _F32), jax.ShapeDtypeStruct((m, n), _BF16)],
        compiler_params=pltpu.CompilerParams(dimension_semantics=("parallel",),
                                             vmem_limit_bytes=_vmem_limit(est)),
        name="out_proj_residual_norm",
    )(a, w, resid, gate, norm_g, sc, sh)


def _merge_kernel(m_ref, cf_ref, wl_ref, wc_ref, gl_ref, gc_ref, o_ref):
    lo = jnp.dot(m_ref[...], wl_ref[...], preferred_element_type=_F32)
    co = jnp.dot(cf_ref[...], wc_ref[...], preferred_element_type=_F32)
    o_ref[...] = (gl_ref[...].astype(_F32) * lo + gc_ref[...].astype(_F32) * co).astype(o_ref.dtype)


def _merge_mm(m_act, cf_act, w_lru, w_conf, gates, *, tm=256, n_col_groups=2, sides=()):
    m, kdim = m_act.shape
    n = w_lru.shape[1]
    tn = n // n_col_groups
    resident = functools.partial(pl.BlockSpec, (kdim, tn), lambda j, i: (0, j), pipeline_mode=pl.Buffered(1))
    est = 4 * _nbytes((tm, kdim), _BF16) + 2 * _nbytes((kdim, tn), _BF16) + 6 * _nbytes((tm, tn), _BF16)
    est += 4 * _nbytes((tm, tn), _F32)
    return _matmul_call(
        _merge_kernel, grid=(n_col_groups, m // tm),
        in_specs=[pl.BlockSpec((tm, kdim), lambda j, i: (i, 0)),
                  pl.BlockSpec((tm, kdim), lambda j, i: (i, 0)),
                  resident(), resident(),
                  pl.BlockSpec((tm, tn), lambda j, i: (i, j)),
                  pl.BlockSpec((tm, tn), lambda j, i: (i, j + n_col_groups))],
        out_spec=pl.BlockSpec((tm, tn), lambda j, i: (i, j)),
        out_shape=jax.ShapeDtypeStruct((m, n), _BF16), scratch=[],
        semantics=("parallel", "parallel"), est=est, name="gated_merge",
        args=[m_act, cf_act, w_lru, w_conf, gates, gates], sides=sides)


def _lru_kernel(x_ref, cw_ref, cb_ref, wa_ref, ba_ref, wi_ref, bi_ref, lam_ref, h0_ref, *rest,
                reverse, combine, tt, heads_per_block):
    if combine:
        hf_ref, gate_ref, o_ref, px, xcp, xedge, carry = rest
    else:
        o_ref, px, xcp, xedge, carry = rest
    cbw = x_ref.shape[1]
    slabs = cbw // _V7X_LANES
    sub = _V7X_SUBLANES
    ln = tt // sub
    pitch = ln + sub
    nhalo = _LRU_CONV - 1
    base = 0 if reverse else nhalo * sub
    i = pl.program_id(2)
    srow = lax.broadcasted_iota(jnp.int32, (sub, _V7X_LANES), 0)

    @pl.when(i == 0)
    def _():
        carry[...] = h0_ref[0]
        xedge[...] = jnp.zeros_like(xedge)

    for s in range(slabs):
        lanes = slice(s * _V7X_LANES, (s + 1) * _V7X_LANES)
        for c in range(sub):
            px[s, c * pitch:c * pitch + ln, :] = x_ref[c * ln:(c + 1) * ln, lanes]
        for j in range(ln):
            xcp[s, base + j * sub:base + (j + 1) * sub, :] = px[s, pl.ds(j, sub, stride=pitch), :]
        for m in range(nhalo):
            prev = xedge[s, m * sub:(m + 1) * sub, :]
            if reverse:
                cur = xcp[s, m * sub:(m + 1) * sub, :]
                xedge[s, m * sub:(m + 1) * sub, :] = cur
                xcp[s, (ln + m) * sub:(ln + m + 1) * sub, :] = pltpu.roll(jnp.where(srow == 0, prev, cur), sub - 1, 0)
            else:
                cur = xcp[s, base + (ln - nhalo + m) * sub:base + (ln - nhalo + m + 1) * sub, :]
                xedge[s, m * sub:(m + 1) * sub, :] = cur
                xcp[s, m * sub:(m + 1) * sub, :] = pltpu.roll(jnp.where(srow == sub - 1, prev, cur), 1, 0)

    def conv_slab(s):
        lanes = slice(s * _V7X_LANES, (s + 1) * _V7X_LANES)
        y = jnp.broadcast_to(cb_ref[:, lanes], (tt, _V7X_LANES))
        for k in range(_LRU_CONV):
            off = (nhalo - k) if reverse else k
            y = y + cw_ref[k:k + 1, lanes] * xcp[s, off * sub:off * sub + tt, :]
        return y

    lam = lam_ref[...]
    z = -lam
    softplus = jnp.maximum(z, 0.0) + jnp.log1p(jnp.exp(-jnp.abs(z)))
    c8 = (-_LRU_C * _LOG2E) * softplus
    hw = cbw // heads_per_block
    order = range(ln - 1, -1, -1) if reverse else range(ln)
    chunk_order = range(sub - 1, -1, -1) if reverse else range(sub)
    for hh in range(heads_per_block):
        cs = slice(hh * hw, (hh + 1) * hw)
        head_slabs = range(hh * hw // _V7X_LANES, (hh + 1) * hw // _V7X_LANES)
        yh = jnp.concatenate([conv_slab(s) for s in head_slabs], axis=1)
        yb = yh.astype(_BF16)
        tr = jnp.tanh(jnp.dot(yb, wa_ref[hh] * 0.5, preferred_element_type=_F32) + 0.5 * ba_ref[:, cs])
        tg = jnp.tanh(jnp.dot(yb, wi_ref[hh] * 0.5, preferred_element_type=_F32) + 0.5 * bi_ref[:, cs])
        c8h = 0.5 * c8[:, cs]
        a = jnp.exp2(c8h * tr + c8h)
        v = 1.0 - a * a
        b = jnp.where(v > 0.0, v * lax.rsqrt(v), 0.0) * (0.5 * tg + 0.5) * yh

        h = jnp.zeros((sub, hw), _F32)
        p = jnp.ones((sub, hw), _F32)
        hs, ps = [None] * ln, [None] * ln
        for j in order:
            aj = a[j * sub:(j + 1) * sub]
            h = aj * h + b[j * sub:(j + 1) * sub]
            p = aj * p
            hs[j], ps[j] = h, p
        init = carry[:, cs]
        inits = [None] * sub
        for c in chunk_order:
            inits[c] = init
            init = p[c:c + 1, :] * init + h[c:c + 1, :]
        carry[:, cs] = init
        init_all = jnp.concatenate(inits, axis=0)
        for j in range(ln):
            hj = ps[j] * init_all + hs[j]
            for n, s in enumerate(head_slabs):
                px[s, pl.ds(j, sub, stride=pitch), :] = hj[:, n * _V7X_LANES:(n + 1) * _V7X_LANES]
        for s in head_slabs:
            lanes = slice(s * _V7X_LANES, (s + 1) * _V7X_LANES)
            for c in range(sub):
                rows = slice(c * ln, (c + 1) * ln)
                hn = px[s, c * pitch:c * pitch + ln, :]
                if combine:
                    hn = (hn + hf_ref[rows, lanes]) * gate_ref[rows, lanes].astype(_F32)
                o_ref[rows, lanes] = hn.astype(o_ref.dtype)


def _lru(x2d, cw, cb, wa, ba, wi, bi, lam, h0, *, batch, reverse, hf=None, gate=None,
         tt=1024, heads_per_block=4):
    rows, r = x2d.shape
    t = rows // batch
    tt = min(tt, t)
    nt = t // tt
    hw = r // _LRU_HEADS
    cbw = hw * heads_per_block
    slabs = cbw // _V7X_LANES
    ln = tt // _V7X_SUBLANES
    combine = hf is not None

    def row_map(b, h, i):
        ti = (nt - 1 - i) if reverse else i
        return (b * nt + ti, h)

    chan = lambda b, h, i: (0, h)
    in_specs = [pl.BlockSpec((tt, cbw), row_map),
                pl.BlockSpec((_LRU_CONV, cbw), chan),
                pl.BlockSpec((1, cbw), chan),
                pl.BlockSpec((heads_per_block, hw, hw), lambda b, h, i: (h, 0, 0)),
                pl.BlockSpec((1, cbw), chan),
                pl.BlockSpec((heads_per_block, hw, hw), lambda b, h, i: (h, 0, 0)),
                pl.BlockSpec((1, cbw), chan),
                pl.BlockSpec((1, cbw), chan),
                pl.BlockSpec((1, 1, cbw), lambda b, h, i: (b, 0, h))]
    args = [x2d, cw, cb, wa, ba, wi, bi, lam, h0]
    if combine:
        in_specs += [pl.BlockSpec((tt, cbw), row_map), pl.BlockSpec((tt, cbw), row_map)]
        args += [hf, gate]
    px_shape = (slabs, _V7X_SUBLANES * (ln + _V7X_SUBLANES), _V7X_LANES)
    xcp_shape = (slabs, (ln + _LRU_CONV - 1) * _V7X_SUBLANES, _V7X_LANES)
    edge_shape = (slabs, (_LRU_CONV - 1) * _V7X_SUBLANES, _V7X_LANES)
    scratch = [pltpu.VMEM(px_shape, _F32), pltpu.VMEM(xcp_shape, _F32), pltpu.VMEM(edge_shape, _F32),
               pltpu.VMEM((1, cbw), _F32)]
    tile = _nbytes((tt, cbw), _F32)
    est = 8 * tile + _nbytes(px_shape, _F32) + _nbytes(xcp_shape, _F32) + 8 * tile
    return pl.pallas_call(
        functools.partial(_lru_kernel, reverse=reverse, combine=combine, tt=tt,
                          heads_per_block=heads_per_block),
        grid=(batch, _LRU_HEADS // heads_per_block, nt),
        in_specs=in_specs,
        out_specs=pl.BlockSpec((tt, cbw), row_map),
        out_shape=jax.ShapeDtypeStruct((rows, r), _BF16 if combine else _F32),
        scratch_shapes=scratch,
        compiler_params=pltpu.CompilerParams(
            dimension_semantics=("parallel", "parallel", "arbitrary"),
            vmem_limit_bytes=_vmem_limit(est)),
        name="rglru_bwd" if reverse else "rglru_fwd",
    )(*args)


def _conf_conv_kernel(v_ref, w_ref, b_ref, o_ref, z, pt, *, n, cb, n_col_blocks, lines_per_step):
    slabs = cb // _V7X_LANES
    pitch = n + _V7X_SUBLANES
    c = pl.program_id(1)
    zeros = jnp.zeros((_CONF_PAD * n, _V7X_LANES), _F32)

    def conv(s, write_line_block):
        def step(jj, carry):
            l0 = jj * lines_per_step
            accs = [jnp.broadcast_to(b_ref[:, s * _V7X_LANES:(s + 1) * _V7X_LANES], (n, _V7X_LANES))] * lines_per_step
            for m in range(lines_per_step + _CONF_K - 1):
                line = z[s, pl.ds(pl.multiple_of((l0 + m) * n, n), n), :]
                for l in range(lines_per_step):
                    k = m - l
                    if 0 <= k < _CONF_K:
                        accs[l] = accs[l] + w_ref[k:k + 1, s * _V7X_LANES:(s + 1) * _V7X_LANES] * line
            write_line_block(l0, accs)
            return carry
        lax.fori_loop(0, n // lines_per_step, step, 0)

    for s in range(slabs):
        lanes = slice(s * _V7X_LANES, (s + 1) * _V7X_LANES)
        z[s, 0:_CONF_PAD * n, :] = zeros
        z[s, (_CONF_PAD + n) * n:, :] = zeros

        @pl.when(c < n_col_blocks)
        def _():
            for i in range(n):
                pt[s, i * pitch:i * pitch + n, :] = v_ref[0, i * n:(i + 1) * n, lanes]
            for j in range(n):
                for i0 in range(n // _V7X_SUBLANES):
                    r0 = (_CONF_PAD + j) * n + i0 * _V7X_SUBLANES
                    z[s, r0:r0 + _V7X_SUBLANES, :] = pt[
                        s, pl.ds(i0 * _V7X_SUBLANES * pitch + j, _V7X_SUBLANES, stride=pitch), :]

            def write(l0, accs):
                for l in range(lines_per_step):
                    pt[s, pl.ds(pl.multiple_of((l0 + l) * pitch, _V7X_SUBLANES), n), :] = accs[l]
            conv(s, write)

            rows_per_store = 2 * _V7X_SUBLANES
            for i in range(n):
                for j0 in range(0, n, rows_per_store):
                    halves = [pt[s, pl.ds((j0 + h) * pitch + i, _V7X_SUBLANES, stride=pitch), :]
                              for h in range(0, rows_per_store, _V7X_SUBLANES)]
                    r0 = i * n + j0
                    o_ref[0, r0:r0 + rows_per_store, lanes] = jnp.concatenate(halves, axis=0).astype(o_ref.dtype)

        @pl.when(c >= n_col_blocks)
        def _():
            z[s, _CONF_PAD * n:(_CONF_PAD + n) * n, :] = v_ref[0, :, lanes]

            def write(l0, accs):
                for l in range(lines_per_step):
                    o_ref[0, pl.ds(pl.multiple_of((l0 + l) * n, n), n), lanes] = accs[l].astype(o_ref.dtype)
            conv(s, write)


def _conf_conv(v3d, w, b, *, n, cb=256, lines_per_step=4):
    bsz, t, ch = v3d.shape
    assert t == n * n and (ch // 2) % cb == 0
    n_col_blocks = (ch // 2) // cb
    slabs = cb // _V7X_LANES
    z_shape = (slabs, (n + 2 * _CONF_PAD) * n, _V7X_LANES)
    pt_shape = (slabs, n * (n + _V7X_SUBLANES), _V7X_LANES)
    est = 4 * _nbytes((t, cb), _F32) + _nbytes(z_shape, _F32) + _nbytes(pt_shape, _F32)
    return pl.pallas_call(
        functools.partial(_conf_conv_kernel, n=n, cb=cb, n_col_blocks=n_col_blocks,
                          lines_per_step=lines_per_step),
        grid=(bsz, ch // cb),
        in_specs=[pl.BlockSpec((1, t, cb), lambda i, c: (i, 0, c)),
                  pl.BlockSpec((_CONF_K, cb), lambda i, c: (0, c)),
                  pl.BlockSpec((1, cb), lambda i, c: (0, c))],
        out_specs=pl.BlockSpec((1, t, cb), lambda i, c: (i, 0, c)),
        out_shape=jax.ShapeDtypeStruct((bsz, t, ch), _F32),
        scratch_shapes=[pltpu.VMEM(z_shape, _F32), pltpu.VMEM(pt_shape, _F32)],
        compiler_params=pltpu.CompilerParams(dimension_semantics=("parallel", "parallel"),
                                             vmem_limit_bytes=_vmem_limit(est)),
        name="conformer_dwconv",
    )(v3d, w, b)


def _ln_silu_kernel(x_ref, g_ref, b_ref, o_ref):
    x = x_ref[...].astype(_F32)
    mu = jnp.mean(x, axis=-1, keepdims=True)
    xc = x - mu
    y = xc * lax.rsqrt(jnp.mean(xc * xc, axis=-1, keepdims=True) + _EPS)
    y = y * g_ref[...] + b_ref[...]
    o_ref[...] = (y * jax.nn.sigmoid(y)).astype(o_ref.dtype)


def _ln_silu(x, g, b, tt=512):
    m, d = x.shape
    est = 2 * _nbytes((tt, d), x.dtype) + 2 * _nbytes((tt, d), _BF16) + 4 * _nbytes((tt, d), _F32)
    return pl.pallas_call(
        _ln_silu_kernel,
        grid=(m // tt,),
        in_specs=[pl.BlockSpec((tt, d), lambda i: (i, 0)),
                  pl.BlockSpec((1, d), lambda i: (0, 0)),
                  pl.BlockSpec((1, d), lambda i: (0, 0))],
        out_specs=pl.BlockSpec((tt, d), lambda i: (i, 0)),
        out_shape=jax.ShapeDtypeStruct((m, d), _BF16),
        compiler_params=pltpu.CompilerParams(dimension_semantics=("parallel",),
                                             vmem_limit_bytes=_vmem_limit(est)),
        name="layernorm_swish",
    )(x, g, b)


def kernel(x, c, ctx, c_ctx, w_ada, b_ada, norm1_g, w_in, lru_conv_w, lru_conv_b, lru_w_a, lru_b_a,
           lru_w_i, lru_b_i, lru_lam, w_lru_out, conf_dw_w, conf_dw_b, conf_ln_g, conf_ln_b,
           w_conf_out, w_o, norm2_g, w_ff1, w_ff2, final_g):
    assert w_ada.shape[0] == 1, "single-layer stack only (the context stream is then never updated)"
    bsz, t, d = x.shape
    tc = ctx.shape[1]
    r = lru_lam.shape[-1]
    cw = conf_dw_w.shape[-1]
    row = lambda v: v.reshape(1, -1)

    cc = jnp.concatenate([c, c_ctx[None, :], jnp.zeros((_V7X_SUBLANES - bsz - 1, d), _F32)], axis=0)
    mod = _ada(cc, w_ada[0], row(b_ada[0]))
    mod_l = mod[:bsz].reshape(bsz, 6, 1, d)
    sh1, sc1, g1, sh2, sc2, g2 = (mod_l[:, k] for k in range(6))
    mod_c = jnp.broadcast_to(mod[bsz].reshape(6, 1, 1, d), (6, bsz, 1, d))
    csh1, csc1 = mod_c[0], mod_c[1]

    u = _norm_mod(x, row(norm1_g[0]), sc1, sh1).reshape(bsz * t, d)
    uc = _norm_mod(ctx, row(norm1_g[0]), csc1, csh1).reshape(bsz * tc, d)

    w_in0 = w_in[0]
    x_br, w_gate_b = _proj(u, w_in0, offs=(r,), n_cols=r, epilogue="none", out_dtype=_F32, tn=512,
                           sides=[(w_in0, 0, r)], name="in_proj_lru")
    gate_act, w_glu_b = _mm(u, w_gate_b, epilogue="gelu", out_dtype=_BF16,
                            sides=[(w_in0, 2 * r, 2 * cw)], name="in_proj_gate")
    v, w_branch_b = _mm(u, w_glu_b, n_cols=cw, epilogue="glu", glu_gate_off=cw, out_dtype=_F32, tn=512,
                        sides=[(w_in0, 2 * r + 2 * cw, 2 * d)], name="in_proj_glu")
    br_gates, w_lru_b, w_conf_b, w_ff1_b = _mm(u, w_branch_b, epilogue="sigmoid", out_dtype=_BF16,
                                               sides=[w_lru_out[0], w_conf_out[0], w_ff1[0]],
                                               name="in_proj_branch")
    xc_br = _proj(uc, w_in0, offs=(r,), n_cols=r, epilogue="none", out_dtype=_F32, tn=512, name="in_proj_ctx")

    def lru_params(dirn):
        return (lru_conv_w[0, dirn], row(lru_conv_b[0, dirn]), lru_w_a[0, dirn].astype(_BF16),
                row(lru_b_a[0, dirn]), lru_w_i[0, dirn].astype(_BF16), row(lru_b_i[0, dirn]),
                row(lru_lam[0, dirn]))

    zero_state = jnp.zeros((bsz, 1, r), _F32)
    hc_f = _lru(xc_br, *lru_params(0), zero_state, batch=bsz, reverse=False)
    hc_b = _lru(xc_br, *lru_params(1), zero_state, batch=bsz, reverse=True)
    h0_f = hc_f.reshape(bsz, tc, r)[:, tc - 1:tc, :]
    h0_b = hc_b.reshape(bsz, tc, r)[:, 0:1, :]
    h_f = _lru(x_br, *lru_params(0), h0_f, batch=bsz, reverse=False)
    lru_act = _lru(x_br, *lru_params(1), h0_b, batch=bsz, reverse=True, hf=h_f, gate=gate_act)

    conv = _conf_conv(v.reshape(bsz, t, cw), conf_dw_w[0], row(conf_dw_b[0]), n=_GRID_W)
    conf_act = _ln_silu(conv.reshape(bsz * t, cw), row(conf_ln_g[0]), row(conf_ln_b[0]))

    merged, w_o_b = _merge_mm(lru_act, conf_act, w_lru_b, w_conf_b, br_gates, sides=[w_o[0]])
    x1, u2 = _out_proj_norm(merged, w_o_b, x.reshape(bsz * t, d), g1, row(norm2_g[0]), sc2, sh2,
                            rows_per_batch=t)
    hid, w_ff2_b = _mm(u2, w_ff1_b, epilogue="relu2", out_dtype=_BF16, sides=[w_ff2[0]], name="mlp_up")
    x2 = _mm(hid, w_ff2_b, epilogue="resid", resid=x1, gate=g2, rows_per_batch=t, tk=4096,
             name="mlp_down_residual")
    return _final_norm(x2, row(final_g)).reshape(bsz, t, d)
```

```python
import functools

import jax
import jax.numpy as jnp
from jax import lax
from jax.experimental import pallas as pl
from jax.experimental.pallas import tpu as pltpu

_F32 = jnp.float32
_BF16 = jnp.bfloat16

_GRID_W = 64
_LRU_HEADS = 16
_LRU_CONV = 4
_LRU_C = 8.0
_CONF_K = 31
_CONF_PAD = (_CONF_K - 1) // 2
_EPS = 1e-6
_LOG2E = 1.4426950408889634

_V7X_SUBLANES = 8
_V7X_LANES = 128
_V7X_SCOPED_VMEM_BYTES = 60000 * 1024


_COMPILER_SCRATCH_BYTES = 2 << 20


def _vmem_limit(nbytes):
    return int(min(nbytes + _COMPILER_SCRATCH_BYTES, _V7X_SCOPED_VMEM_BYTES))


def _nbytes(shape, dtype):
    n = 1
    for s in shape:
        n *= s
    return n * jnp.dtype(dtype).itemsize


def _sigmoid(x):
    return 0.5 * jnp.tanh(0.5 * x) + 0.5


def _ada_kernel(cc_ref, w_ref, b_ref, o_ref):
    cc = cc_ref[...]
    s = cc * _sigmoid(cc)
    o_ref[...] = jnp.dot(s.astype(_BF16), w_ref[...].astype(_BF16),
                         preferred_element_type=_F32) + b_ref[...]


def _ada(cc, w, b, tn=512):
    m, d = cc.shape
    n = w.shape[1]
    est = 2 * _nbytes((d, tn), _F32) + 2 * _nbytes((d, tn), _BF16) + 4 * _nbytes((m, d), _F32)
    return pl.pallas_call(
        _ada_kernel,
        grid=(n // tn,),
        in_specs=[pl.BlockSpec((m, d), lambda j: (0, 0)),
                  pl.BlockSpec((d, tn), lambda j: (0, j)),
                  pl.BlockSpec((1, tn), lambda j: (0, j))],
        out_specs=pl.BlockSpec((m, tn), lambda j: (0, j)),
        out_shape=jax.ShapeDtypeStruct((m, n), _F32),
        compiler_params=pltpu.CompilerParams(dimension_semantics=("parallel",),
                                             vmem_limit_bytes=_vmem_limit(est)),
        name="ada_ln",
    )(cc, w, b)


def _norm_mod_kernel(x_ref, g_ref, sc_ref, sh_ref, o_ref):
    x = x_ref[0]
    ms = jnp.mean(x * x, axis=-1, keepdims=True)
    y = x * lax.rsqrt(ms + _EPS) * g_ref[...]
    o_ref[0] = (y * (1.0 + sc_ref[0]) + sh_ref[0]).astype(o_ref.dtype)


def _norm_mod(x, g, sc, sh, tt=512):
    b, t, d = x.shape
    tt = min(tt, t)
    est = 2 * _nbytes((tt, d), _F32) + 2 * _nbytes((tt, d), _BF16) + 2 * _nbytes((tt, d), _F32)
    return pl.pallas_call(
        _norm_mod_kernel,
        grid=(b, t // tt),
        in_specs=[pl.BlockSpec((1, tt, d), lambda i, j: (i, j, 0)),
                  pl.BlockSpec((1, d), lambda i, j: (0, 0)),
                  pl.BlockSpec((1, 1, d), lambda i, j: (i, 0, 0)),
                  pl.BlockSpec((1, 1, d), lambda i, j: (i, 0, 0))],
        out_specs=pl.BlockSpec((1, tt, d), lambda i, j: (i, j, 0)),
        out_shape=jax.ShapeDtypeStruct((b, t, d), _BF16),
        compiler_params=pltpu.CompilerParams(dimension_semantics=("parallel", "parallel"),
                                             vmem_limit_bytes=_vmem_limit(est)),
        name="rmsnorm_modulate",
    )(x, g, sc, sh)


def _final_norm_kernel(x_ref, g_ref, o_ref):
    x = x_ref[...]
    ms = jnp.mean(x * x, axis=-1, keepdims=True)
    o_ref[...] = x * lax.rsqrt(ms + _EPS) * g_ref[...]


def _final_norm(x, g, tt=512):
    m, d = x.shape
    est = 4 * _nbytes((tt, d), _F32) + 2 * _nbytes((tt, d), _F32)
    return pl.pallas_call(
        _final_norm_kernel,
        grid=(m // tt,),
        in_specs=[pl.BlockSpec((tt, d), lambda i: (i, 0)),
                  pl.BlockSpec((1, d), lambda i: (0, 0))],
        out_specs=pl.BlockSpec((tt, d), lambda i: (i, 0)),
        out_shape=jax.ShapeDtypeStruct((m, d), _F32),
        compiler_params=pltpu.CompilerParams(dimension_semantics=("parallel",),
                                             vmem_limit_bytes=_vmem_limit(est)),
        name="final_rmsnorm",
    )(x, g)


def _side_cast_body(body, n_in, n_side, *refs):
    side_ins = refs[n_in:n_in + n_side]
    o_ref = refs[n_in + n_side]
    side_outs = refs[n_in + n_side + 1:n_in + 2 * n_side + 1]
    for s_in, s_out in zip(side_ins, side_outs):
        s_out[...] = s_in[...].astype(_BF16)
    body(*refs[:n_in], o_ref, *refs[n_in + 2 * n_side + 1:])


def _matmul_call(body, *, grid, in_specs, out_spec, out_shape, scratch, semantics, est, name, args, sides=()):
    if not sides:
        return pl.pallas_call(
            body, grid=grid, in_specs=in_specs, out_specs=out_spec, out_shape=out_shape, scratch_shapes=scratch,
            compiler_params=pltpu.CompilerParams(dimension_semantics=semantics, vmem_limit_bytes=_vmem_limit(est)),
            name=name)(*args)
    n_steps = 1
    for g in grid:
        n_steps *= g

    def linear_step(idx):
        lin = 0
        for g, ix in zip(grid, idx):
            lin = lin * g + ix
        return lin

    side_args, side_in_specs, side_out_specs, side_shapes = [], [], [], []
    for side in sides:
        arr, col0, cols = side if isinstance(side, tuple) else (side, 0, side.shape[1])
        rows = arr.shape[0]
        rb = rows // n_steps
        assert rb * n_steps == rows and rb % (2 * _V7X_SUBLANES) == 0 and col0 % cols == 0, (arr.shape, grid)
        side_args.append(arr)
        side_in_specs.append(pl.BlockSpec((rb, cols), functools.partial(
            lambda *idx, cb: (linear_step(idx), cb), cb=col0 // cols)))
        side_out_specs.append(pl.BlockSpec((rb, cols), lambda *idx: (linear_step(idx), 0)))
        side_shapes.append(jax.ShapeDtypeStruct((rows, cols), _BF16))
        est += 2 * _nbytes((rb, cols), _F32) + 2 * _nbytes((rb, cols), _BF16)
    return pl.pallas_call(
        functools.partial(_side_cast_body, body, len(in_specs), len(sides)),
        grid=grid, in_specs=in_specs + side_in_specs, out_specs=[out_spec] + side_out_specs,
        out_shape=[out_shape] + side_shapes, scratch_shapes=scratch,
        compiler_params=pltpu.CompilerParams(dimension_semantics=semantics, vmem_limit_bytes=_vmem_limit(est)),
        name=name)(*args, *side_args)


def _epilogue(name, acc, x_ref=None, g_ref=None):
    if name == "none":
        return acc
    if name == "gelu":
        return jax.nn.gelu(acc)
    if name == "sigmoid":
        return _sigmoid(acc)
    if name == "relu2":
        return jnp.square(jnp.maximum(acc, 0.0))
    if name == "resid":
        return x_ref[...] + g_ref[0] * acc
    raise ValueError(name)


def _mm_kernel(a_ref, *rest, epilogue, nk, n_w):
    w_refs, rest = rest[:n_w], rest[n_w:]
    if epilogue == "resid":
        x_ref, g_ref, o_ref = rest[:3]
        scratch = rest[3:]
    else:
        x_ref = g_ref = None
        o_ref = rest[0]
        scratch = rest[1:]
    if nk == 1:
        a = a_ref[...]
        accs = [jnp.dot(a, w_ref[...], preferred_element_type=_F32) for w_ref in w_refs]
        if epilogue == "glu":
            r = accs[0] * _sigmoid(accs[1])
        else:
            r = _epilogue(epilogue, accs[0], x_ref, g_ref)
        o_ref[...] = r.astype(o_ref.dtype)
        return
    acc_ref = scratch[0] if scratch else o_ref
    k = pl.program_id(2)

    @pl.when(k == 0)
    def _():
        acc_ref[...] = jnp.zeros_like(acc_ref)

    acc_ref[...] += jnp.dot(a_ref[...], w_refs[0][...], preferred_element_type=_F32)

    @pl.when(k == nk - 1)
    def _():
        o_ref[...] = _epilogue(epilogue, acc_ref[...], x_ref, g_ref).astype(o_ref.dtype)


def _mm(a, w, *, n_off=0, n_cols=None, epilogue="none", out_dtype=_F32, resid=None, gate=None,
        rows_per_batch=None, glu_gate_off=None, tm=1024, tn=1024, tk=None, sides=(), name="matmul"):
    m, kdim = a.shape
    n_cols = w.shape[1] if n_cols is None else n_cols
    tk = kdim if tk is None else tk
    tm = min(tm, m)
    nk = kdim // tk
    col_offs = (n_off, glu_gate_off) if epilogue == "glu" else (n_off,)
    n_w = len(col_offs)
    assert n_w == 1 or nk == 1
    grid = (m // tm, n_cols // tn, nk)
    in_specs = [pl.BlockSpec((tm, tk), lambda i, j, k: (i, k))]
    in_specs += [pl.BlockSpec((tk, tn), functools.partial(lambda i, j, k, jo: (k, j + jo), jo=off // tn))
                 for off in col_offs]
    args = [a] + [w] * n_w
    est = 2 * _nbytes((tm, tk), a.dtype) + 2 * n_w * _nbytes((tk, tn), w.dtype)
    est += 2 * _nbytes((tm, tn), out_dtype) + (2 + n_w) * _nbytes((tm, tn), _F32)
    if epilogue == "resid":
        tiles_per_batch = rows_per_batch // tm
        in_specs += [pl.BlockSpec((tm, tn), lambda i, j, k: (i, j)),
                     pl.BlockSpec((1, 1, tn), lambda i, j, k: (i // tiles_per_batch, 0, j))]
        args += [resid, gate]
        est += 2 * _nbytes((tm, tn), _F32)
    scratch = []
    if nk > 1 and out_dtype != _F32:
        scratch = [pltpu.VMEM((tm, tn), _F32)]
        est += _nbytes((tm, tn), _F32)
    return _matmul_call(
        functools.partial(_mm_kernel, epilogue=epilogue, nk=nk, n_w=n_w), grid=grid, in_specs=in_specs,
        out_spec=pl.BlockSpec((tm, tn), lambda i, j, k: (i, j)),
        out_shape=jax.ShapeDtypeStruct((m, n_cols), out_dtype), scratch=scratch,
        semantics=("parallel", "parallel", "arbitrary"), est=est, name=name, args=args, sides=sides)


def _proj_kernel(a_ref, *rest, epilogue, n_w):
    w_refs, o_ref, wb_refs = rest[:n_w], rest[n_w], rest[n_w + 1:]

    @pl.when(pl.program_id(1) == 0)
    def _():
        for w_ref, wb in zip(w_refs, wb_refs):
            wb[...] = w_ref[...].astype(_BF16)

    a = a_ref[...]
    accs = [jnp.dot(a, wb[...], preferred_element_type=_F32) for wb in wb_refs]
    if epilogue == "glu":
        r = accs[0] * _sigmoid(accs[1])
    else:
        r = _epilogue(epilogue, accs[0])
    o_ref[...] = r.astype(o_ref.dtype)


def _proj(a, w, *, offs, n_cols, epilogue, out_dtype, tn, tm=1024, sides=(), name):
    m, kdim = a.shape
    tm = min(tm, m)
    n_w = len(offs)
    w_specs = [pl.BlockSpec((kdim, tn), functools.partial(lambda j, i, jo: (0, j + jo), jo=off // tn))
               for off in offs]
    est = 2 * _nbytes((tm, kdim), a.dtype) + n_w * (2 * _nbytes((kdim, tn), _F32) + _nbytes((kdim, tn), _BF16))
    est += 2 * _nbytes((tm, tn), out_dtype) + (n_w + 1) * _nbytes((tm, tn), _F32)
    return _matmul_call(
        functools.partial(_proj_kernel, epilogue=epilogue, n_w=n_w), grid=(n_cols // tn, m // tm),
        in_specs=[pl.BlockSpec((tm, kdim), lambda j, i: (i, 0))] + w_specs,
        out_spec=pl.BlockSpec((tm, tn), lambda j, i: (i, j)),
        out_shape=jax.ShapeDtypeStruct((m, n_cols), out_dtype),
        scratch=[pltpu.VMEM((kdim, tn), _BF16)] * n_w, semantics=("parallel", "arbitrary"), est=est,
        name=name, args=[a] + [w] * n_w, sides=sides)


def _resident_proj_kernel(a_ref, w_ref, o_ref, *, epilogue):
    acc = jnp.dot(a_ref[...], w_ref[...], preferred_element_type=_F32)
    o_ref[...] = _epilogue(epilogue, acc).astype(o_ref.dtype)


def _resident_proj(a, w, *, epilogue, out_dtype, tm, n_col_groups, sides=(), name):
    m, kdim = a.shape
    n = w.shape[1]
    tn = n // n_col_groups
    est = 2 * _nbytes((tm, kdim), a.dtype) + _nbytes((kdim, tn), w.dtype) + 2 * _nbytes((tm, tn), out_dtype)
    est += 2 * _nbytes((tm, tn), _F32)
    return _matmul_call(
        functools.partial(_resident_proj_kernel, epilogue=epilogue), grid=(n_col_groups, m // tm),
        in_specs=[pl.BlockSpec((tm, kdim), lambda j, i: (i, 0)),
                  pl.BlockSpec((kdim, tn), lambda j, i: (0, j), pipeline_mode=pl.Buffered(1))],
        out_spec=pl.BlockSpec((tm, tn), lambda j, i: (i, j)),
        out_shape=jax.ShapeDtypeStruct((m, n), out_dtype), scratch=[],
        semantics=("parallel", "parallel"), est=est, name=name, args=[a, w], sides=sides)


def _out_proj_norm_kernel(a_ref, w_ref, x_ref, g_ref, ng_ref, sc_ref, sh_ref, o_ref, u_ref):
    acc = jnp.dot(a_ref[...], w_ref[...], preferred_element_type=_F32)
    x1 = x_ref[...] + g_ref[0] * acc
    o_ref[...] = x1
    ms = jnp.mean(x1 * x1, axis=-1, keepdims=True)
    y = x1 * lax.rsqrt(ms + _EPS) * ng_ref[...]
    u_ref[...] = (y * (1.0 + sc_ref[0]) + sh_ref[0]).astype(u_ref.dtype)


def _out_proj_norm(a, w, resid, gate, norm_g, sc, sh, *, rows_per_batch, tm=256):
    m, kdim = a.shape
    n = w.shape[1]
    tiles_per_batch = rows_per_batch // tm
    per_batch = pl.BlockSpec((1, 1, n), lambda i: (i // tiles_per_batch, 0, 0))
    est = 2 * _nbytes((tm, kdim), a.dtype) + _nbytes((kdim, n), w.dtype) + 4 * _nbytes((tm, n), _F32)
    est += 2 * _nbytes((tm, n), _BF16) + 3 * _nbytes((tm, n), _F32)
    return pl.pallas_call(
        _out_proj_norm_kernel,
        grid=(m // tm,),
        in_specs=[pl.BlockSpec((tm, kdim), lambda i: (i, 0)),
                  pl.BlockSpec((kdim, n), lambda i: (0, 0), pipeline_mode=pl.Buffered(1)),
                  pl.BlockSpec((tm, n), lambda i: (i, 0)),
                  per_batch,
                  pl.BlockSpec((1, n), lambda i: (0, 0)),
                  per_batch,
                  per_batch],
        out_specs=[pl.BlockSpec((tm, n), lambda i: (i, 0)), pl.BlockSpec((tm, n), lambda i: (i, 0))],
        out_shape=[jax.ShapeDtypeStruct((m, n), _F32), jax.ShapeDtypeStruct((m, n), _BF16)],
        compiler_params=pltpu.CompilerParams(dimension_semantics=("parallel",),
                                             vmem_limit_bytes=_vmem_limit(est)),
        name="out_proj_residual_norm",
    )(a, w, resid, gate, norm_g, sc, sh)


def _merge_kernel(m_ref, cf_ref, wl_ref, wc_ref, gl_ref, gc_ref, o_ref):
    lo = jnp.dot(m_ref[...], wl_ref[...], preferred_element_type=_F32)
    co = jnp.dot(cf_ref[...], wc_ref[...], preferred_element_type=_F32)
    o_ref[...] = (gl_ref[...].astype(_F32) * lo + gc_ref[...].astype(_F32) * co).astype(o_ref.dtype)


def _merge_mm(m_act, cf_act, w_lru, w_conf, gates, *, tm=256, n_col_groups=2, sides=()):
    m, kdim = m_act.shape
    n = w_lru.shape[1]
    tn = n // n_col_groups
    resident = functools.partial(pl.BlockSpec, (kdim, tn), lambda j, i: (0, j), pipeline_mode=pl.Buffered(1))
    est = 4 * _nbytes((tm, kdim), _BF16) + 2 * _nbytes((kdim, tn), _BF16) + 6 * _nbytes((tm, tn), _BF16)
    est += 4 * _nbytes((tm, tn), _F32)
    return _matmul_call(
        _merge_kernel, grid=(n_col_groups, m // tm),
        in_specs=[pl.BlockSpec((tm, kdim), lambda j, i: (i, 0)),
                  pl.BlockSpec((tm, kdim), lambda j, i: (i, 0)),
                  resident(), resident(),
                  pl.BlockSpec((tm, tn), lambda j, i: (i, j)),
                  pl.BlockSpec((tm, tn), lambda j, i: (i, j + n_col_groups))],
        out_spec=pl.BlockSpec((tm, tn), lambda j, i: (i, j)),
        out_shape=jax.ShapeDtypeStruct((m, n), _BF16), scratch=[],
        semantics=("parallel", "parallel"), est=est, name="gated_merge",
        args=[m_act, cf_act, w_lru, w_conf, gates, gates], sides=sides)


def _lru_kernel(x_ref, cw_ref, cb_ref, wa_ref, ba_ref, wi_ref, bi_ref, lam_ref, h0_ref, *rest,
                reverse, combine, tt, heads_per_block):
    if combine:
        hf_ref, gate_ref, o_ref, px, xcp, xedge, carry = rest
    else:
        o_ref, px, xcp, xedge, carry = rest
    cbw = x_ref.shape[1]
    slabs = cbw // _V7X_LANES
    sub = _V7X_SUBLANES
    ln = tt // sub
    pitch = ln + sub
    nhalo = _LRU_CONV - 1
    base = 0 if reverse else nhalo * sub
    i = pl.program_id(2)
    srow = lax.broadcasted_iota(jnp.int32, (sub, _V7X_LANES), 0)

    @pl.when(i == 0)
    def _():
        carry[...] = h0_ref[0]
        xedge[...] = jnp.zeros_like(xedge)

    for s in range(slabs):
        lanes = slice(s * _V7X_LANES, (s + 1) * _V7X_LANES)
        for c in range(sub):
            px[s, c * pitch:c * pitch + ln, :] = x_ref[c * ln:(c + 1) * ln, lanes]
        for j in range(ln):
            xcp[s, base + j * sub:base + (j + 1) * sub, :] = px[s, pl.ds(j, sub, stride=pitch), :]
        for m in range(nhalo):
            prev = xedge[s, m * sub:(m + 1) * sub, :]
            if reverse:
                cur = xcp[s, m * sub:(m + 1) * sub, :]
                xedge[s, m * sub:(m + 1) * sub, :] = cur
                xcp[s, (ln + m) * sub:(ln + m + 1) * sub, :] = pltpu.roll(jnp.where(srow == 0, prev, cur), sub - 1, 0)
            else:
                cur = xcp[s, base + (ln - nhalo + m) * sub:base + (ln - nhalo + m + 1) * sub, :]
                xedge[s, m * sub:(m + 1) * sub, :] = cur
                xcp[s, m * sub:(m + 1) * sub, :] = pltpu.roll(jnp.where(srow == sub - 1, prev, cur), 1, 0)

    def conv_slab(s):
        lanes = slice(s * _V7X_LANES, (s + 1) * _V7X_LANES)
        y = jnp.broadcast_to(cb_ref[:, lanes], (tt, _V7X_LANES))
        for k in range(_LRU_CONV):
            off = (nhalo - k) if reverse else k
            y = y + cw_ref[k:k + 1, lanes] * xcp[s, off * sub:off * sub + tt, :]
        return y

    lam = lam_ref[...]
    z = -lam
    softplus = jnp.maximum(z, 0.0) + jnp.log1p(jnp.exp(-jnp.abs(z)))
    c8 = (-_LRU_C * _LOG2E) * softplus
    hw = cbw // heads_per_block
    order = range(ln - 1, -1, -1) if reverse else range(ln)
    chunk_order = range(sub - 1, -1, -1) if reverse else range(sub)
    for hh in range(heads_per_block):
        cs = slice(hh * hw, (hh + 1) * hw)
        head_slabs = range(hh * hw // _V7X_LANES, (hh + 1) * hw // _V7X_LANES)
        yh = jnp.concatenate([conv_slab(s) for s in head_slabs], axis=1)
        yb = yh.astype(_BF16)
        tr = jnp.tanh(jnp.dot(yb, wa_ref[hh] * 0.5, preferred_element_type=_F32) + 0.5 * ba_ref[:, cs])
        tg = jnp.tanh(jnp.dot(yb, wi_ref[hh] * 0.5, preferred_element_type=_F32) + 0.5 * bi_ref[:, cs])
        c8h = 0.5 * c8[:, cs]
        a = jnp.exp2(c8h * tr + c8h)
        v = 1.0 - a * a
        b = jnp.where(v > 0.0, v * lax.rsqrt(v), 0.0) * (0.5 * tg + 0.5) * yh

        h = jnp.zeros((sub, hw), _F32)
        p = jnp.ones((sub, hw), _F32)
        hs, ps = [None] * ln, [None] * ln
        for j in order:
            aj = a[j * sub:(j + 1) * sub]
            h = aj * h + b[j * sub:(j + 1) * sub]
            p = aj * p
            hs[j], ps[j] = h, p
        init = carry[:, cs]
        inits = [None] * sub
        for c in chunk_order:
            inits[c] = init
            init = p[c:c + 1, :] * init + h[c:c + 1, :]
        carry[:, cs] = init
        init_all = jnp.concatenate(inits, axis=0)
        for j in range(ln):
            hj = ps[j] * init_all + hs[j]
            for n, s in enumerate(head_slabs):
                px[s, pl.ds(j, sub, stride=pitch), :] = hj[:, n * _V7X_LANES:(n + 1) * _V7X_LANES]
        for s in head_slabs:
            lanes = slice(s * _V7X_LANES, (s + 1) * _V7X_LANES)
            for c in range(sub):
                rows = slice(c * ln, (c + 1) * ln)
                hn = px[s, c * pitch:c * pitch + ln, :]
                if combine:
                    hn = (hn + hf_ref[rows, lanes]) * gate_ref[rows, lanes].astype(_F32)
                o_ref[rows, lanes] = hn.astype(o_ref.dtype)


def _lru(x2d, cw, cb, wa, ba, wi, bi, lam, h0, *, batch, reverse, hf=None, gate=None,
         tt=1024, heads_per_block=4):
    rows, r = x2d.shape
    t = rows // batch
    tt = min(tt, t)
    nt = t // tt
    hw = r // _LRU_HEADS
    cbw = hw * heads_per_block
    slabs = cbw // _V7X_LANES
    ln = tt // _V7X_SUBLANES
    combine = hf is not None

    def row_map(b, h, i):
        ti = (nt - 1 - i) if reverse else i
        return (b * nt + ti, h)

    chan = lambda b, h, i: (0, h)
    in_specs = [pl.BlockSpec((tt, cbw), row_map),
                pl.BlockSpec((_LRU_CONV, cbw), chan),
                pl.BlockSpec((1, cbw), chan),
                pl.BlockSpec((heads_per_block, hw, hw), lambda b, h, i: (h, 0, 0)),
                pl.BlockSpec((1, cbw), chan),
                pl.BlockSpec((heads_per_block, hw, hw), lambda b, h, i: (h, 0, 0)),
                pl.BlockSpec((1, cbw), chan),
                pl.BlockSpec((1, cbw), chan),
                pl.BlockSpec((1, 1, cbw), lambda b, h, i: (b, 0, h))]
    args = [x2d, cw, cb, wa, ba, wi, bi, lam, h0]
    if combine:
        in_specs += [pl.BlockSpec((tt, cbw), row_map), pl.BlockSpec((tt, cbw), row_map)]
        args += [hf, gate]
    px_shape = (slabs, _V7X_SUBLANES * (ln + _V7X_SUBLANES), _V7X_LANES)
    xcp_shape = (slabs, (ln + _LRU_CONV - 1) * _V7X_SUBLANES, _V7X_LANES)
    edge_shape = (slabs, (_LRU_CONV - 1) * _V7X_SUBLANES, _V7X_LANES)
    scratch = [pltpu.VMEM(px_shape, _F32), pltpu.VMEM(xcp_shape, _F32), pltpu.VMEM(edge_shape, _F32),
               pltpu.VMEM((1, cbw), _F32)]
    tile = _nbytes((tt, cbw), _F32)
    est = 8 * tile + _nbytes(px_shape, _F32) + _nbytes(xcp_shape, _F32) + 8 * tile
    return pl.pallas_call(
        functools.partial(_lru_kernel, reverse=reverse, combine=combine, tt=tt,
                          heads_per_block=heads_per_block),
        grid=(batch, _LRU_HEADS // heads_per_block, nt),
        in_specs=in_specs,
        out_specs=pl.BlockSpec((tt, cbw), row_map),
        out_shape=jax.ShapeDtypeStruct((rows, r), _BF16 if combine else _F32),
        scratch_shapes=scratch,
        compiler_params=pltpu.CompilerParams(
            dimension_semantics=("parallel", "parallel", "arbitrary"),
            vmem_limit_bytes=_vmem_limit(est)),
        name="rglru_bwd" if reverse else "rglru_fwd",
    )(*args)


def _conf_conv_kernel(v_ref, w_ref, b_ref, o_ref, z, pt, *, n, cb, n_col_blocks, lines_per_step):
    slabs = cb // _V7X_LANES
    pitch = n + _V7X_SUBLANES
    c = pl.program_id(1)
    zeros = jnp.zeros((_CONF_PAD * n, _V7X_LANES), _F32)

    def conv(s, write_line_block):
        def step(jj, carry):
            l0 = jj * lines_per_step
            accs = [jnp.broadcast_to(b_ref[:, s * _V7X_LANES:(s + 1) * _V7X_LANES], (n, _V7X_LANES))] * lines_per_step
            for m in range(lines_per_step + _CONF_K - 1):
                line = z[s, pl.ds(pl.multiple_of((l0 + m) * n, n), n), :]
                for l in range(lines_per_step):
                    k = m - l
                    if 0 <= k < _CONF_K:
                        accs[l] = accs[l] + w_ref[k:k + 1, s * _V7X_LANES:(s + 1) * _V7X_LANES] * line
            write_line_block(l0, accs)
            return carry
        lax.fori_loop(0, n // lines_per_step, step, 0)

    for s in range(slabs):
        lanes = slice(s * _V7X_LANES, (s + 1) * _V7X_LANES)
        z[s, 0:_CONF_PAD * n, :] = zeros
        z[s, (_CONF_PAD + n) * n:, :] = zeros

        @pl.when(c < n_col_blocks)
        def _():
            for i in range(n):
                pt[s, i * pitch:i * pitch + n, :] = v_ref[0, i * n:(i + 1) * n, lanes]
            for j in range(n):
                for i0 in range(n // _V7X_SUBLANES):
                    r0 = (_CONF_PAD + j) * n + i0 * _V7X_SUBLANES
                    z[s, r0:r0 + _V7X_SUBLANES, :] = pt[
                        s, pl.ds(i0 * _V7X_SUBLANES * pitch + j, _V7X_SUBLANES, stride=pitch), :]

            def write(l0, accs):
                for l in range(lines_per_step):
                    pt[s, pl.ds(pl.multiple_of((l0 + l) * pitch, _V7X_SUBLANES), n), :] = accs[l]
            conv(s, write)

            rows_per_store = 2 * _V7X_SUBLANES
            for i in range(n):
                for j0 in range(0, n, rows_per_store):
                    halves = [pt[s, pl.ds((j0 + h) * pitch + i, _V7X_SUBLANES, stride=pitch), :]
                              for h in range(0, rows_per_store, _V7X_SUBLANES)]
                    r0 = i * n + j0
                    o_ref[0, r0:r0 + rows_per_store, lanes] = jnp.concatenate(halves, axis=0).astype(o_ref.dtype)

        @pl.when(c >= n_col_blocks)
        def _():
            z[s, _CONF_PAD * n:(_CONF_PAD + n) * n, :] = v_ref[0, :, lanes]

            def write(l0, accs):
                for l in range(lines_per_step):
                    o_ref[0, pl.ds(pl.multiple_of((l0 + l) * n, n), n), lanes] = accs[l].astype(o_ref.dtype)
            conv(s, write)


def _conf_conv(v3d, w, b, *, n, cb=256, lines_per_step=4):
    bsz, t, ch = v3d.shape
    assert t == n * n and (ch // 2) % cb == 0
    n_col_blocks = (ch // 2) // cb
    slabs = cb // _V7X_LANES
    z_shape = (slabs, (n + 2 * _CONF_PAD) * n, _V7X_LANES)
    pt_shape = (slabs, n * (n + _V7X_SUBLANES), _V7X_LANES)
    est = 4 * _nbytes((t, cb), _F32) + _nbytes(z_shape, _F32) + _nbytes(pt_shape, _F32)
    return pl.pallas_call(
        functools.partial(_conf_conv_kernel, n=n, cb=cb, n_col_blocks=n_col_blocks,
                          lines_per_step=lines_per_step),
        grid=(bsz, ch // cb),
        in_specs=[pl.BlockSpec((1, t, cb), lambda i, c: (i, 0, c)),
                  pl.BlockSpec((_CONF_K, cb), lambda i, c: (0, c)),
                  pl.BlockSpec((1, cb), lambda i, c: (0, c))],
        out_specs=pl.BlockSpec((1, t, cb), lambda i, c: (i, 0, c)),
        out_shape=jax.ShapeDtypeStruct((bsz, t, ch), _F32),
        scratch_shapes=[pltpu.VMEM(z_shape, _F32), pltpu.VMEM(pt_shape, _F32)],
        compiler_params=pltpu.CompilerParams(dimension_semantics=("parallel", "parallel"),
                                             vmem_limit_bytes=_vmem_limit(est)),
        name="conformer_dwconv",
    )(v3d, w, b)


def _ln_silu_kernel(x_ref, g_ref, b_ref, o_ref):
    x = x_ref[...].astype(_F32)
    mu = jnp.mean(x, axis=-1, keepdims=True)
    xc = x - mu
    y = xc * lax.rsqrt(jnp.mean(xc * xc, axis=-1, keepdims=True) + _EPS)
    y = y * g_ref[...] + b_ref[...]
    o_ref[...] = (y * jax.nn.sigmoid(y)).astype(o_ref.dtype)


def _ln_silu(x, g, b, tt=512):
    m, d = x.shape
    est = 2 * _nbytes((tt, d), x.dtype) + 2 * _nbytes((tt, d), _BF16) + 4 * _nbytes((tt, d), _F32)
    return pl.pallas_call(
        _ln_silu_kernel,
        grid=(m // tt,),
        in_specs=[pl.BlockSpec((tt, d), lambda i: (i, 0)),
                  pl.BlockSpec((1, d), lambda i: (0, 0)),
                  pl.BlockSpec((1, d), lambda i: (0, 0))],
        out_specs=pl.BlockSpec((tt, d), lambda i: (i, 0)),
        out_shape=jax.ShapeDtypeStruct((m, d), _BF16),
        compiler_params=pltpu.CompilerParams(dimension_semantics=("parallel",),
                                             vmem_limit_bytes=_vmem_limit(est)),
        name="layernorm_swish",
    )(x, g, b)


def kernel(x, c, ctx, c_ctx, w_ada, b_ada, norm1_g, w_in, lru_conv_w, lru_conv_b, lru_w_a, lru_b_a,
           lru_w_i, lru_b_i, lru_lam, w_lru_out, conf_dw_w, conf_dw_b, conf_ln_g, conf_ln_b,
           w_conf_out, w_o, norm2_g, w_ff1, w_ff2, final_g):
    assert w_ada.shape[0] == 1, "single-layer stack only (the context stream is then never updated)"
    bsz, t, d = x.shape
    tc = ctx.shape[1]
    r = lru_lam.shape[-1]
    cw = conf_dw_w.shape[-1]
    row = lambda v: v.reshape(1, -1)

    cc = jnp.concatenate([c, c_ctx[None, :], jnp.zeros((_V7X_SUBLANES - bsz - 1, d), _F32)], axis=0)
    mod = _ada(cc, w_ada[0], row(b_ada[0]))
    mod_l = mod[:bsz].reshape(bsz, 6, 1, d)
    sh1, sc1, g1, sh2, sc2, g2 = (mod_l[:, k] for k in range(6))
    mod_c = jnp.broadcast_to(mod[bsz].reshape(6, 1, 1, d), (6, bsz, 1, d))
    csh1, csc1 = mod_c[0], mod_c[1]

    u = _norm_mod(x, row(norm1_g[0]), sc1, sh1).reshape(bsz * t, d)
    uc = _norm_mod(ctx, row(norm1_g[0]), csc1, csh1).reshape(bsz * tc, d)

    w_in0 = w_in[0]
    x_br, w_gate_b = _proj(u, w_in0, offs=(r,), n_cols=r, epilogue="none", out_dtype=_F32, tn=512,
                           sides=[(w_in0, 0, r)], name="in_proj_lru")
    gate_act, w_glu_b = _mm(u, w_gate_b, epilogue="gelu", out_dtype=_BF16,
                            sides=[(w_in0, 2 * r, 2 * cw)], name="in_proj_gate")
    v, w_branch_b, w_ff1_b = _mm(u, w_glu_b, n_cols=cw, epilogue="glu", glu_gate_off=cw, out_dtype=_F32, tn=512,
                                 sides=[(w_in0, 2 * r + 2 * cw, 2 * d), w_ff1[0]], name="in_proj_glu")
    br_gates, w_lru_b, w_conf_b = _resident_proj(u, w_branch_b, epilogue="sigmoid", out_dtype=_BF16, tm=512,
                                                 n_col_groups=2, sides=[w_lru_out[0], w_conf_out[0]],
                                                 name="in_proj_branch")
    xc_br = _proj(uc, w_in0, offs=(r,), n_cols=r, epilogue="none", out_dtype=_F32, tn=512, name="in_proj_ctx")

    def lru_params(dirn):
        return (lru_conv_w[0, dirn], row(lru_conv_b[0, dirn]), lru_w_a[0, dirn].astype(_BF16),
                row(lru_b_a[0, dirn]), lru_w_i[0, dirn].astype(_BF16), row(lru_b_i[0, dirn]),
                row(lru_lam[0, dirn]))

    zero_state = jnp.zeros((bsz, 1, r), _F32)
    hc_f = _lru(xc_br, *lru_params(0), zero_state, batch=bsz, reverse=False)
    hc_b = _lru(xc_br, *lru_params(1), zero_state, batch=bsz, reverse=True)
    h0_f = hc_f.reshape(bsz, tc, r)[:, tc - 1:tc, :]
    h0_b = hc_b.reshape(bsz, tc, r)[:, 0:1, :]
    h_f = _lru(x_br, *lru_params(0), h0_f, batch=bsz, reverse=False)
    lru_act = _lru(x_br, *lru_params(1), h0_b, batch=bsz, reverse=True, hf=h_f, gate=gate_act)

    conv = _conf_conv(v.reshape(bsz, t, cw), conf_dw_w[0], row(conf_dw_b[0]), n=_GRID_W)
    conf_act = _ln_silu(conv.reshape(bsz * t, cw), row(conf_ln_g[0]), row(conf_ln_b[0]))

    merged, w_o_b = _merge_mm(lru_act, conf_act, w_lru_b, w_conf_b, br_gates, sides=[w_o[0]])
    x1, u2 = _out_proj_norm(merged, w_o_b, x.reshape(bsz * t, d), g1, row(norm2_g[0]), sc2, sh2,
                            rows_per_batch=t)
    hid, w_ff2_b = _mm(u2, w_ff1_b, epilogue="relu2", out_dtype=_BF16, sides=[w_ff2[0]], name="mlp_up")
    x2 = _mm(hid, w_ff2_b, epilogue="resid", resid=x1, gate=g2, rows_per_batch=t, tk=4096,
             name="mlp_down_residual")
    return _final_norm(x2, row(final_g)).reshape(bsz, t, d)
```
